```python
import jax, jax.numpy as jnp
from jax import lax
import numpy as np

D_MODEL = 1024
BATCH = 2
SEQ = 8192
DEPTH = 2
DEC_BATCH = 128
DEC_SEQ = 4
PAST_LEN = 8192
PAGE_SIZE = 128

N_BRANCH = 4
BRANCH_W = D_MODEL // 4
SC_WIDTH = 3
GM_CHUNK = 128
GM_GROUPS = 4
GM_GROUP_W = BRANCH_W // GM_GROUPS
MLA_HEADS = 4
QK_NOPE = BRANCH_W // MLA_HEADS
QK_ROPE = QK_NOPE // 2
V_HEAD = BRANCH_W // MLA_HEADS
Q_RANK = D_MODEL // 4
KV_RANK = D_MODEL // 8
ROPE_THETA = 10000.0
ATTN_BLOCK = 128
CF_WIDTH = 31
D_FF = ((8 * D_MODEL // 3 + 255) // 256) * 256
RMS_EPS = 1e-6
LN_EPS = 1e-5

IN_SIZES = (BRANCH_W, BRANCH_W, BRANCH_W,
            BRANCH_W, BRANCH_W,
            Q_RANK, KV_RANK, QK_ROPE,
            BRANCH_W, BRANCH_W,
            N_BRANCH * D_MODEL)
IN_COLS = sum(IN_SIZES)

kernel_name = 'hybrid_parallel_gated_mla_conv_gmlp_decode_step'


def rms_norm(x, g):
    xf = x.astype(jnp.float32)
    y = xf * lax.rsqrt(jnp.mean(xf * xf, axis=-1, keepdims=True) + RMS_EPS)
    return (y * g.astype(jnp.float32)).astype(x.dtype)


def layer_norm(x, g, b):
    xf = x.astype(jnp.float32)
    mu = jnp.mean(xf, axis=-1, keepdims=True)
    var = jnp.mean(jnp.square(xf - mu), axis=-1, keepdims=True)
    y = (xf - mu) * lax.rsqrt(var + LN_EPS)
    return (y * g.astype(jnp.float32) + b.astype(jnp.float32)).astype(x.dtype)


def swiglu(x, wg, wu, wd):
    return (jax.nn.silu(x @ wg) * (x @ wu)) @ wd


def rope(x, pos):
    half = x.shape[-1] // 2
    inv = ROPE_THETA ** (-jnp.arange(half, dtype=jnp.float32) / half)
    ang = pos.astype(jnp.float32)[:, None] * inv[None, :]
    shape = (1, pos.shape[0]) + (1,) * (x.ndim - 3) + (half,)
    cos = jnp.cos(ang).reshape(shape)
    sin = jnp.sin(ang).reshape(shape)
    x1 = x[..., :half].astype(jnp.float32)
    x2 = x[..., half:].astype(jnp.float32)
    return jnp.concatenate([x1 * cos - x2 * sin, x2 * cos + x1 * sin], axis=-1).astype(x.dtype)


def causal_dwconv(u, buf, w):
    k = w.shape[0]
    full = jnp.concatenate([buf.astype(u.dtype), u], axis=1)
    out = lax.conv_general_dilated(full, w[:, None, :].astype(u.dtype), window_strides=(1,),
                                   padding='VALID', dimension_numbers=('NWC', 'WIO', 'NWC'),
                                   feature_group_count=u.shape[-1])
    return out, full[:, full.shape[1] - (k - 1):]


def spatial_gate_mix(v, ws, bs):
    n, t, _ = v.shape
    c = min(t, GM_CHUNK)
    wm = (ws * jnp.tril(jnp.ones((GM_CHUNK, GM_CHUNK), ws.dtype)))[:, :c, :c]
    vc = v.reshape(n, t // c, c, GM_GROUPS, GM_GROUP_W)
    out = jnp.einsum('gij,nkjgw->nkigw', wm, vc) + bs[:, :c].T[None, None, :, :, None]
    return out.reshape(n, t, BRANCH_W)


def causal_block_attention(q, k, v, scale):
    n, t, h, e = q.shape
    nb = t // ATTN_BLOCK
    qb = q.reshape(n, nb, ATTN_BLOCK, h, e).transpose(1, 0, 2, 3, 4)
    kpos = jnp.arange(t)

    def block(args):
        qi, bi = args
        qpos = bi * ATTN_BLOCK + jnp.arange(ATTN_BLOCK)
        s = jnp.einsum('nqhe,nkhe->nhqk', qi, k).astype(jnp.float32) * scale
        s = jnp.where(kpos[None, :] <= qpos[:, None], s, -jnp.inf)
        p = jax.nn.softmax(s, axis=-1).astype(v.dtype)
        return jnp.einsum('nhqk,nkhd->nqhd', p, v)

    o = lax.map(block, (qb, jnp.arange(nb)))
    return o.transpose(1, 0, 2, 3, 4).reshape(n, t, h, v.shape[-1])


def latent_decode_attention(q_nope, q_pe, lat, kr, past_lat, past_kr, w_uk, w_uv, scale):
    t = q_nope.shape[1]
    p_len = past_lat.shape[1]
    q_lat = jnp.einsum('nthd,chd->nthc', q_nope, w_uk)
    s_past = (jnp.einsum('nthc,npc->nhtp', q_lat, past_lat)
              + jnp.einsum('nthr,npr->nhtp', q_pe, past_kr)).astype(jnp.float32) * scale
    s_new = (jnp.einsum('nthc,nsc->nhts', q_lat, lat)
             + jnp.einsum('nthr,nsr->nhts', q_pe, kr)).astype(jnp.float32) * scale
    s_new = jnp.where(jnp.tril(jnp.ones((t, t), bool))[None, None], s_new, -jnp.inf)
    prob = jax.nn.softmax(jnp.concatenate([s_past, s_new], axis=-1), axis=-1).astype(lat.dtype)
    o_lat = (jnp.einsum('nhtp,npc->nthc', prob[..., :p_len], past_lat)
             + jnp.einsum('nhts,nsc->nthc', prob[..., p_len:], lat))
    return jnp.einsum('nthc,chd->nthd', o_lat, w_uv)


def mla(c_q, c_kv, c_kr, pos, past_lat, past_kr, lp):
    n, t, _ = c_q.shape
    q = jnp.einsum('ntr,rhe->nthe', rms_norm(c_q, lp['q_norm']), lp['w_uq'])
    q_nope = q[..., :QK_NOPE]
    q_pe = rope(q[..., QK_NOPE:], pos)
    lat = rms_norm(c_kv, lp['kv_norm'])
    kr = rope(c_kr, pos)
    scale = (QK_NOPE + QK_ROPE) ** -0.5
    if past_lat is None:
        k_nope = jnp.einsum('ntc,chd->nthd', lat, lp['w_uk'])
        vv = jnp.einsum('ntc,chd->nthd', lat, lp['w_uv'])
        k = jnp.concatenate([k_nope, jnp.broadcast_to(kr[:, :, None, :], (n, t, MLA_HEADS, QK_ROPE))], axis=-1)
        o = causal_block_attention(jnp.concatenate([q_nope, q_pe], axis=-1), k, vv, scale)
    else:
        o = latent_decode_attention(q_nope, q_pe, lat, kr, past_lat, past_kr, lp['w_uk'], lp['w_uv'], scale)
    return o.reshape(n, t, MLA_HEADS * V_HEAD), lat, kr


def split_points():
    pts, acc = [], 0
    for s in IN_SIZES[:-1]:
        acc += s
        pts.append(acc)
    return pts


def token_mixer(xn, pos, buf_a, buf_d, past_lat, past_kr, lp):
    n, t, _ = xn.shape
    z = xn @ lp['w_in']
    (a_h, a_c, a_b, b_u, b_v, c_q, c_kv, c_kr, d_val, d_gate, gate_logits) = jnp.split(z, split_points(), axis=-1)
    a_conv, new_buf_a = causal_dwconv(a_c * a_h, buf_a, lp['sc_w'])
    y_a = a_b * a_conv
    u = jax.nn.gelu(b_u)
    v = rms_norm(jax.nn.gelu(b_v), lp['gm_norm'])
    y_b = u * spatial_gate_mix(v, lp['gm_ws'], lp['gm_bs'])
    y_c, lat, kr = mla(c_q, c_kv, c_kr, pos, past_lat, past_kr, lp)
    d_conv, new_buf_d = causal_dwconv(d_val * jax.nn.sigmoid(d_gate), buf_d, lp['cf_w'])
    y_d = jax.nn.silu(layer_norm(d_conv + lp['cf_b'], lp['cf_ln_g'], lp['cf_ln_b']))
    ys = jnp.stack([y_a, y_b, y_c, y_d], axis=2)
    branch = jnp.einsum('ntiw,iwd->ntid', ys, lp['w_branch'])
    gates = jax.nn.sigmoid(gate_logits.reshape(n, t, N_BRANCH, D_MODEL))
    merged = jnp.sum(gates * branch, axis=2)
    return merged @ lp['w_out'], new_buf_a, new_buf_d, lat, kr, v


def decoder_layer(h, pos, buf_a, buf_d, past_lat, past_kr, lp):
    h = h + 0.5 * swiglu(rms_norm(h, lp['n1']), lp['f1_g'], lp['f1_u'], lp['f1_d'])
    m, nba, nbd, lat, kr, v = token_mixer(rms_norm(h, lp['nm']), pos, buf_a, buf_d, past_lat, past_kr, lp)
    h = h + m
    h = h + 0.5 * swiglu(rms_norm(h, lp['n2']), lp['f2_g'], lp['f2_u'], lp['f2_d'])
    return h, nba, nbd, lat, kr, v


def setup_inputs(seed: int = 0) -> dict:
    key = jax.random.key(seed)
    ks = iter(jax.random.split(key, 48))
    f32 = jnp.float32

    def nrm(shape, scale):
        return jax.random.normal(next(ks), shape, f32) * scale

    def gain(shape):
        return 1.0 + 0.02 * jax.random.normal(next(ks), shape, f32)

    n_pages = PAST_LEN // PAGE_SIZE
    n_used = DEC_BATCH * n_pages
    n_pool = n_used + max(1, n_used // 4)
    perm = jax.random.permutation(next(ks), n_pool)
    page_table = perm[:n_used].reshape(DEC_BATCH, n_pages).astype(jnp.int32)
    return {
        'x_prompt': nrm((BATCH, SEQ, D_MODEL), 1.0),
        'x_sample': nrm((DEC_BATCH, DEC_SEQ, D_MODEL), 1.0),
        'cache_mla_latent': nrm((DEPTH, n_pool, PAGE_SIZE, KV_RANK), 1.0),
        'cache_mla_krope': nrm((DEPTH, n_pool, PAGE_SIZE, QK_ROPE), 1.0),
        'page_table': page_table,
        'state_conv_a': nrm((DEPTH, DEC_BATCH, SC_WIDTH - 1, BRANCH_W), 1.0),
        'state_conv_d': nrm((DEPTH, DEC_BATCH, CF_WIDTH - 1, BRANCH_W), 0.5),
        'ffn1_norm': gain((DEPTH, D_MODEL)),
        'ffn1_w_gate': nrm((DEPTH, D_MODEL, D_FF), D_MODEL ** -0.5),
        'ffn1_w_up': nrm((DEPTH, D_MODEL, D_FF), D_MODEL ** -0.5),
        'ffn1_w_down': nrm((DEPTH, D_FF, D_MODEL), D_FF ** -0.5),
        'mix_norm': gain((DEPTH, D_MODEL)),
        'w_in': nrm((DEPTH, D_MODEL, IN_COLS), D_MODEL ** -0.5),
        'sc_conv_w': nrm((DEPTH, SC_WIDTH, BRANCH_W), SC_WIDTH ** -0.5),
        'gm_norm': gain((DEPTH, BRANCH_W)),
        'gm_spatial_w': nrm((DEPTH, GM_GROUPS, GM_CHUNK, GM_CHUNK), GM_CHUNK ** -0.5),
        'gm_spatial_b': nrm((DEPTH, GM_GROUPS, GM_CHUNK), 0.02),
        'mla_q_norm': gain((DEPTH, Q_RANK)),
        'mla_kv_norm': gain((DEPTH, KV_RANK)),
        'mla_w_uq': nrm((DEPTH, Q_RANK, MLA_HEADS, QK_NOPE + QK_ROPE), Q_RANK ** -0.5),
        'mla_w_uk': nrm((DEPTH, KV_RANK, MLA_HEADS, QK_NOPE), KV_RANK ** -0.5),
        'mla_w_uv': nrm((DEPTH, KV_RANK, MLA_HEADS, V_HEAD), KV_RANK ** -0.5),
        'cf_dw_w': nrm((DEPTH, CF_WIDTH, BRANCH_W), CF_WIDTH ** -0.5),
        'cf_dw_b': nrm((DEPTH, BRANCH_W), 0.02),
        'cf_ln_g': gain((DEPTH, BRANCH_W)),
        'cf_ln_b': nrm((DEPTH, BRANCH_W), 0.02),
        'w_branch': nrm((DEPTH, N_BRANCH, BRANCH_W, D_MODEL), BRANCH_W ** -0.5),
        'w_out': nrm((DEPTH, D_MODEL, D_MODEL), D_MODEL ** -0.5),
        'ffn2_norm': gain((DEPTH, D_MODEL)),
        'ffn2_w_gate': nrm((DEPTH, D_MODEL, D_FF), D_MODEL ** -0.5),
        'ffn2_w_up': nrm((DEPTH, D_MODEL, D_FF), D_MODEL ** -0.5),
        'ffn2_w_down': nrm((DEPTH, D_FF, D_MODEL), D_FF ** -0.5),
        'final_norm': gain((D_MODEL,)),
    }


def reference(x_prompt, x_sample, cache_mla_latent, cache_mla_krope, page_table, state_conv_a, state_conv_d,
              ffn1_norm, ffn1_w_gate, ffn1_w_up, ffn1_w_down, mix_norm, w_in, sc_conv_w, gm_norm,
              gm_spatial_w, gm_spatial_b, mla_q_norm, mla_kv_norm, mla_w_uq, mla_w_uk, mla_w_uv,
              cf_dw_w, cf_dw_b, cf_ln_g, cf_ln_b, w_branch, w_out, ffn2_norm, ffn2_w_gate, ffn2_w_up,
              ffn2_w_down, final_norm):
    n_p, t_p, _ = x_prompt.shape
    n_s, t_s, _ = x_sample.shape
    past = page_table.shape[1] * PAGE_SIZE
    pos_p = jnp.arange(t_p, dtype=jnp.int32)
    pos_s = past + jnp.arange(t_s, dtype=jnp.int32)
    zero_a = jnp.zeros((n_p, SC_WIDTH - 1, BRANCH_W), x_prompt.dtype)
    zero_d = jnp.zeros((n_p, CF_WIDTH - 1, BRANCH_W), x_prompt.dtype)

    h_p, h_s = x_prompt, x_sample
    lat_p_l, kr_p_l, ca_p_l, cd_p_l = [], [], [], []
    lat_s_l, kr_s_l, ca_s_l, cd_s_l, v_s_l = [], [], [], [], []
    for l in range(DEPTH):
        lp = dict(n1=ffn1_norm[l], f1_g=ffn1_w_gate[l], f1_u=ffn1_w_up[l], f1_d=ffn1_w_down[l],
                  nm=mix_norm[l], w_in=w_in[l], sc_w=sc_conv_w[l], gm_norm=gm_norm[l],
                  gm_ws=gm_spatial_w[l], gm_bs=gm_spatial_b[l], q_norm=mla_q_norm[l],
                  kv_norm=mla_kv_norm[l], w_uq=mla_w_uq[l], w_uk=mla_w_uk[l], w_uv=mla_w_uv[l],
                  cf_w=cf_dw_w[l], cf_b=cf_dw_b[l], cf_ln_g=cf_ln_g[l], cf_ln_b=cf_ln_b[l],
                  w_branch=w_branch[l], w_out=w_out[l], n2=ffn2_norm[l], f2_g=ffn2_w_gate[l],
                  f2_u=ffn2_w_up[l], f2_d=ffn2_w_down[l])
        h_p, ca_p, cd_p, lat_p, kr_p, _ = decoder_layer(h_p, pos_p, zero_a, zero_d, None, None, lp)
        past_lat = cache_mla_latent[l][page_table].reshape(n_s, past, KV_RANK)
        past_kr = cache_mla_krope[l][page_table].reshape(n_s, past, QK_ROPE)
        h_s, ca_s, cd_s, lat_s, kr_s, v_s = decoder_layer(h_s, pos_s, state_conv_a[l], state_conv_d[l],
                                                          past_lat, past_kr, lp)
        lat_p_l.append(lat_p); kr_p_l.append(kr_p); ca_p_l.append(ca_p); cd_p_l.append(cd_p)
        lat_s_l.append(lat_s); kr_s_l.append(kr_s); ca_s_l.append(ca_s); cd_s_l.append(cd_s)
        v_s_l.append(v_s)

    y_prompt = rms_norm(h_p, final_norm)
    y_sample = rms_norm(h_s, final_norm)
    return (y_prompt, y_sample,
            jnp.stack(lat_p_l), jnp.stack(kr_p_l), jnp.stack(ca_p_l), jnp.stack(cd_p_l),
            jnp.stack(lat_s_l), jnp.stack(kr_s_l), jnp.stack(ca_s_l), jnp.stack(cd_s_l),
            jnp.stack(v_s_l))
```

```python
import functools

import jax
import jax.numpy as jnp
from jax import lax
from jax.experimental import pallas as pl
from jax.experimental.pallas import tpu as pltpu

F32 = jnp.float32
BF16 = jnp.bfloat16

D_MODEL = 1024
BRANCH_W = D_MODEL // 4
SC_WIDTH = 3
GM_CHUNK = 128
GM_GROUPS = 4
GM_GROUP_W = BRANCH_W // GM_GROUPS
MLA_HEADS = 4
QK_NOPE = BRANCH_W // MLA_HEADS
QK_ROPE = QK_NOPE // 2
V_HEAD = BRANCH_W // MLA_HEADS
Q_RANK = D_MODEL // 4
KV_RANK = D_MODEL // 8
ROPE_THETA = 10000.0
CF_WIDTH = 31
D_FF = ((8 * D_MODEL // 3 + 255) // 256) * 256
PAGE_SIZE = 128
RMS_EPS = 1e-6
LN_EPS = 1e-5
ATTN_SCALE = (QK_NOPE + QK_ROPE) ** -0.5

LANES = 128
HEAD_PAD = LANES
TOKEN_TILE = 512
FF_CHUNK = 256
ATTN_TILE = 512
PAGES_PER_STEP = 16
HALO_A = 8
HALO_D = 32
VMEM_LIMIT = 52 * 1024 * 1024

C_AH, C_AC, C_AB, C_BU, C_BV, C_CQ, C_DV, C_DG = (i * BRANCH_W for i in range(8))
C_KV = 8 * BRANCH_W
C_KX = C_KV + KV_RANK
C_KY = C_KX + LANES
W1_COLS = C_KY + LANES


def _rms(x, g):
    return x * lax.rsqrt(jnp.mean(x * x, axis=-1, keepdims=True) + RMS_EPS) * g


def _dot(a, b):
    return jnp.dot(a, b, preferred_element_type=F32)


def _dot_nt(a, b):
    return lax.dot_general(a, b, (((1,), (1,)), ((), ())), preferred_element_type=F32)


def _const_spec(shape):
    nd = len(shape)
    return pl.BlockSpec(shape, lambda *_: (0,) * nd, pipeline_mode=pl.Buffered(1))


def _params(sem):
    return pltpu.CompilerParams(dimension_semantics=sem, vmem_limit_bytes=VMEM_LIMIT)


def _ffn_kernel(*refs, final):
    if final:
        h_ref, g_ref, wg_ref, wu_ref, wd_ref, fg_ref, o_ref = refs
    else:
        h_ref, g_ref, wg_ref, wu_ref, wd_ref, o_ref = refs
    h = h_ref[...]
    xn = _rms(h, g_ref[...]).astype(BF16)
    acc = jnp.zeros_like(h)
    for c in range(0, D_FF, FF_CHUNK):
        gate = _dot(xn, wg_ref[:, c:c + FF_CHUNK])
        up = _dot(xn, wu_ref[:, c:c + FF_CHUNK])
        act = (jax.nn.silu(gate) * up).astype(BF16)
        acc = acc + _dot(act, wd_ref[c:c + FF_CHUNK, :])
    out = h + 0.5 * acc
    if final:
        out = _rms(out, fg_ref[...])
    o_ref[...] = out


def _ffn(h, gain, wg, wu, wd, final_gain=None):
    rows = h.shape[0]
    tm = min(TOKEN_TILE, rows)
    final = final_gain is not None
    row_spec = pl.BlockSpec((tm, D_MODEL), lambda i: (i, 0))
    in_specs = [row_spec, _const_spec((1, D_MODEL)), _const_spec((D_MODEL, D_FF)),
                _const_spec((D_MODEL, D_FF)), _const_spec((D_FF, D_MODEL))]
    args = [h, gain, wg, wu, wd]
    if final:
        in_specs.append(_const_spec((1, D_MODEL)))
        args.append(final_gain)
    return pl.pallas_call(
        functools.partial(_ffn_kernel, final=final),
        grid=(rows // tm,),
        in_specs=in_specs,
        out_specs=row_spec,
        out_shape=jax.ShapeDtypeStruct((rows, D_MODEL), F32),
        compiler_params=_params(("arbitrary",)),
        name="ffn_final" if final else "ffn",
    )(*args)


def _merge_kernel(h_ref, yabd_ref, yc_ref, g_ref, wgate_ref, wba_ref, wbb_ref, wbc_ref, wbd_ref,
                  wout_ref, o_ref):
    h = h_ref[...]
    xn = _rms(h, g_ref[...]).astype(BF16)
    ys = (yabd_ref[:, 0:BRANCH_W], yabd_ref[:, BRANCH_W:2 * BRANCH_W], yc_ref[...],
          yabd_ref[:, 2 * BRANCH_W:3 * BRANCH_W])
    wbs = (wba_ref, wbb_ref, wbc_ref, wbd_ref)
    merged = jnp.zeros_like(h)
    for i in range(4):
        logits = _dot(xn, wgate_ref[:, i * D_MODEL:(i + 1) * D_MODEL])
        merged = merged + jax.nn.sigmoid(logits) * _dot(ys[i], wbs[i][...])
    o_ref[...] = h + _dot(merged.astype(BF16), wout_ref[...])


def _merge(h, yabd, yc, gain, wgate, wba, wbb, wbc, wbd, wout):
    rows = h.shape[0]
    tm = min(TOKEN_TILE, rows)
    yc_w = MLA_HEADS * HEAD_PAD
    row_spec = pl.BlockSpec((tm, D_MODEL), lambda i: (i, 0))
    return pl.pallas_call(
        _merge_kernel,
        grid=(rows // tm,),
        in_specs=[row_spec,
                  pl.BlockSpec((tm, 3 * BRANCH_W), lambda i: (i, 0)),
                  pl.BlockSpec((tm, yc_w), lambda i: (i, 0)),
                  _const_spec((1, D_MODEL)),
                  _const_spec((D_MODEL, 4 * D_MODEL)),
                  _const_spec((BRANCH_W, D_MODEL)), _const_spec((BRANCH_W, D_MODEL)),
                  _const_spec((yc_w, D_MODEL)), _const_spec((BRANCH_W, D_MODEL)),
                  _const_spec((D_MODEL, D_MODEL))],
        out_specs=row_spec,
        out_shape=jax.ShapeDtypeStruct((rows, D_MODEL), F32),
        compiler_params=_params(("arbitrary",)),
        name="merge",
    )(h, yabd, yc, gain, wgate, wba, wbb, wbc, wbd, wout)


def _layer_norm_silu(x, g, b):
    mu = jnp.mean(x, axis=-1, keepdims=True)
    xc = x - mu
    var = jnp.mean(xc * xc, axis=-1, keepdims=True)
    return jax.nn.silu(xc * lax.rsqrt(var + LN_EPS) * g + b)


def _rope_heads(qa, qb, ta, tb):
    return jnp.concatenate(
        [qa[:, hd * HEAD_PAD:(hd + 1) * HEAD_PAD] * ta + qb[:, hd * HEAD_PAD:(hd + 1) * HEAD_PAD] * tb
         for hd in range(MLA_HEADS)], axis=1)


def _mix_in_prompt_kernel(h_ref, g_ref, w1_ref, scw_ref, gmn_ref, gmw_ref, gmb_ref, qn_ref, kvn_ref,
                          wqa_ref, wqb_ref, wkv_ref, cfw_ref, cfb_ref, lng_ref, lnb_ref,
                          tqa_ref, tqb_ref, tka_ref, tkb_ref,
                          q_ref, k_ref, v_ref, yabd_ref, lat_ref, kr_ref, sta_ref, std_ref,
                          bufa, bufd, *, tm):
    @pl.when(pl.program_id(1) == 0)
    def _():
        bufa[0:HALO_A, :] = jnp.zeros((HALO_A, BRANCH_W), F32)
        bufd[0:HALO_D, :] = jnp.zeros((HALO_D, BRANCH_W), F32)

    xn = _rms(h_ref[...], g_ref[...]).astype(BF16)

    za = _dot(xn, w1_ref[:, C_AH:C_BU])
    ga = za[:, BRANCH_W:2 * BRANCH_W] * za[:, 0:BRANCH_W]
    bufa[HALO_A:HALO_A + tm, :] = ga
    conv_a = (scw_ref[0:1, :] * bufa[HALO_A - 2:HALO_A - 2 + tm, :]
              + scw_ref[1:2, :] * bufa[HALO_A - 1:HALO_A - 1 + tm, :]
              + scw_ref[2:3, :] * ga)
    yabd_ref[:, 0:BRANCH_W] = (za[:, 2 * BRANCH_W:3 * BRANCH_W] * conv_a).astype(BF16)
    sta_ref[0] = bufa[HALO_A + tm - (SC_WIDTH - 1):HALO_A + tm, :]
    bufa[0:HALO_A, :] = bufa[tm:tm + HALO_A, :]

    zb = _dot(xn, w1_ref[:, C_BU:C_CQ])
    u = jax.nn.gelu(zb[:, 0:BRANCH_W])
    v = _rms(jax.nn.gelu(zb[:, BRANCH_W:2 * BRANCH_W]), gmn_ref[...]).astype(BF16)
    tril = (lax.broadcasted_iota(jnp.int32, (GM_CHUNK, GM_CHUNK), 0)
            >= lax.broadcasted_iota(jnp.int32, (GM_CHUNK, GM_CHUNK), 1))
    wms = [jnp.where(tril, gmw_ref[g], 0.0).astype(BF16) for g in range(GM_GROUPS)]
    group = lax.broadcasted_iota(jnp.int32, (GM_CHUNK, BRANCH_W), 1) // GM_GROUP_W
    for c in range(tm // GM_CHUNK):
        rows = slice(c * GM_CHUNK, (c + 1) * GM_CHUNK)
        vc = v[rows, :]
        mix = gmb_ref[...]
        for g in range(GM_GROUPS):
            mix = mix + _dot(wms[g], jnp.where(group == g, vc, jnp.zeros_like(vc)))
        yabd_ref[rows, BRANCH_W:2 * BRANCH_W] = (u[rows, :] * mix).astype(BF16)

    zc = _dot(xn, w1_ref[:, C_CQ:C_DV])
    cq = _rms(zc, qn_ref[...]).astype(BF16)
    q = _rope_heads(_dot(cq, wqa_ref[...]), _dot(cq, wqb_ref[...]), tqa_ref[...], tqb_ref[...])
    q_ref[...] = q.astype(BF16)
    zk = _dot(xn, w1_ref[:, C_KV:W1_COLS])
    lat = _rms(zk[:, 0:KV_RANK], kvn_ref[...])
    kr = zk[:, KV_RANK:KV_RANK + LANES] * tka_ref[...] + zk[:, KV_RANK + LANES:] * tkb_ref[...]
    lat_ref[...] = lat
    kr_ref[...] = kr[:, 0:QK_ROPE]
    kv = _dot(jnp.concatenate([lat, kr], axis=1).astype(BF16), wkv_ref[...])
    k_ref[...] = kv[:, 0:MLA_HEADS * HEAD_PAD].astype(BF16)
    v_ref[...] = kv[:, MLA_HEADS * HEAD_PAD:].astype(BF16)

    zd = _dot(xn, w1_ref[:, C_DV:C_KV])
    glu = zd[:, 0:BRANCH_W] * jax.nn.sigmoid(zd[:, BRANCH_W:2 * BRANCH_W])
    bufd[HALO_D:HALO_D + tm, :] = glu
    off = HALO_D - (CF_WIDTH - 1)
    conv_d = cfb_ref[...] + cfw_ref[0:1, :] * bufd[off:off + tm, :]
    for kk in range(1, CF_WIDTH):
        conv_d = conv_d + cfw_ref[kk:kk + 1, :] * bufd[off + kk:off + kk + tm, :]
    yabd_ref[:, 2 * BRANCH_W:3 * BRANCH_W] = _layer_norm_silu(conv_d, lng_ref[...], lnb_ref[...]).astype(BF16)
    std_ref[0] = bufd[HALO_D + tm - (CF_WIDTH - 1):HALO_D + tm, :]
    bufd[0:HALO_D, :] = bufd[tm:tm + HALO_D, :]


def _mix_in_prompt(h, n_seq, seq, lw, tabs):
    tm = min(TOKEN_TILE, seq)
    nt = seq // tm
    rows = n_seq * seq
    hp = MLA_HEADS * HEAD_PAD

    def rowspec(w):
        return pl.BlockSpec((tm, w), lambda b, j: (b * nt + j, 0))

    def tabspec():
        return pl.BlockSpec((tm, LANES), lambda b, j: (j, 0))

    in_specs = [rowspec(D_MODEL), _const_spec((1, D_MODEL)), _const_spec((D_MODEL, W1_COLS)),
                _const_spec((SC_WIDTH, BRANCH_W)), _const_spec((1, BRANCH_W)),
                _const_spec((GM_GROUPS, GM_CHUNK, GM_CHUNK)), _const_spec((GM_CHUNK, BRANCH_W)),
                _const_spec((1, Q_RANK)), _const_spec((1, KV_RANK)),
                _const_spec((Q_RANK, hp)), _const_spec((Q_RANK, hp)), _const_spec((KV_RANK + LANES, 2 * hp)),
                _const_spec((CF_WIDTH, BRANCH_W)), _const_spec((1, BRANCH_W)),
                _const_spec((1, BRANCH_W)), _const_spec((1, BRANCH_W)),
                tabspec(), tabspec(), tabspec(), tabspec()]
    out_specs = [rowspec(hp), rowspec(hp), rowspec(hp), rowspec(3 * BRANCH_W), rowspec(KV_RANK),
                 rowspec(QK_ROPE),
                 pl.BlockSpec((1, SC_WIDTH - 1, BRANCH_W), lambda b, j: (b, 0, 0)),
                 pl.BlockSpec((1, CF_WIDTH - 1, BRANCH_W), lambda b, j: (b, 0, 0))]
    out_shape = [jax.ShapeDtypeStruct((rows, hp), BF16), jax.ShapeDtypeStruct((rows, hp), BF16),
                 jax.ShapeDtypeStruct((rows, hp), BF16), jax.ShapeDtypeStruct((rows, 3 * BRANCH_W), BF16),
                 jax.ShapeDtypeStruct((rows, KV_RANK), F32), jax.ShapeDtypeStruct((rows, QK_ROPE), F32),
                 jax.ShapeDtypeStruct((n_seq, SC_WIDTH - 1, BRANCH_W), F32),
                 jax.ShapeDtypeStruct((n_seq, CF_WIDTH - 1, BRANCH_W), F32)]
    return pl.pallas_call(
        functools.partial(_mix_in_prompt_kernel, tm=tm),
        grid=(n_seq, nt),
        in_specs=in_specs,
        out_specs=out_specs,
        out_shape=out_shape,
        scratch_shapes=[pltpu.VMEM((HALO_A + tm, BRANCH_W), F32), pltpu.VMEM((HALO_D + tm, BRANCH_W), F32)],
        compiler_params=_params(("arbitrary", "arbitrary")),
        name="mix_in_prompt",
    )(h, lw["mix_gain"], lw["w1"], lw["sc_w"], lw["gm_gain"], lw["gm_w"], lw["gm_bias_full"],
      lw["q_gain"], lw["kv_gain"], lw["wqa"], lw["wqb"], lw["wkv"], lw["cf_w"], lw["cf_b"],
      lw["ln_g"], lw["ln_b"], tabs["tqa"], tabs["tqb"], tabs["tka"], tabs["tkb"])


def _mix_in_sample_kernel(h_ref, g_ref, w1_ref, scw_ref, gmn_ref, coef_ref, gmb_ref, qn_ref, kvn_ref,
                          wqa_ref, wqb_ref, wpa_ref, wpb_ref, wukt_ref, cfw_ref, cfb_ref, lng_ref, lnb_ref,
                          tqa_ref, tqb_ref, tka_ref, tkb_ref, sta_ref, std_ref,
                          qlat_ref, qpe_ref, yabd_ref, lat_ref, kr_ref, krp_ref, nsta_ref, glu_ref, v_ref,
                          *, ns, ts):
    def blk(x, t):
        return x[t * ns:(t + 1) * ns, :]

    xn = _rms(h_ref[...], g_ref[...]).astype(BF16)

    za = _dot(xn, w1_ref[:, C_AH:C_BU])
    ga = za[:, BRANCH_W:2 * BRANCH_W] * za[:, 0:BRANCH_W]
    full_a = [sta_ref[i] for i in range(SC_WIDTH - 1)] + [blk(ga, t) for t in range(ts)]
    for t in range(ts):
        conv = scw_ref[0:1, :] * full_a[t]
        for kk in range(1, SC_WIDTH):
            conv = conv + scw_ref[kk:kk + 1, :] * full_a[t + kk]
        yabd_ref[t * ns:(t + 1) * ns, 0:BRANCH_W] = (blk(za, t)[:, 2 * BRANCH_W:3 * BRANCH_W] * conv).astype(BF16)
    for i in range(SC_WIDTH - 1):
        nsta_ref[i] = full_a[ts + i]

    zb = _dot(xn, w1_ref[:, C_BU:C_CQ])
    u = jax.nn.gelu(zb[:, 0:BRANCH_W])
    v = _rms(jax.nn.gelu(zb[:, BRANCH_W:2 * BRANCH_W]), gmn_ref[...])
    v_ref[...] = v
    for i in range(ts):
        mix = gmb_ref[i:i + 1, :] + coef_ref[i * ts:i * ts + 1, :] * blk(v, 0)
        for j in range(1, i + 1):
            mix = mix + coef_ref[i * ts + j:i * ts + j + 1, :] * blk(v, j)
        yabd_ref[i * ns:(i + 1) * ns, BRANCH_W:2 * BRANCH_W] = (blk(u, i) * mix).astype(BF16)

    zc = _dot(xn, w1_ref[:, C_CQ:C_DV])
    cq = _rms(zc, qn_ref[...]).astype(BF16)
    q = _rope_heads(_dot(cq, wqa_ref[...]), _dot(cq, wqb_ref[...]), tqa_ref[...], tqb_ref[...]).astype(BF16)
    qlat_ref[...] = jnp.concatenate(
        [_dot(q[:, hd * HEAD_PAD:(hd + 1) * HEAD_PAD], wukt_ref[hd]) for hd in range(MLA_HEADS)],
        axis=1).astype(BF16)
    qpe = _rope_heads(_dot(cq, wpa_ref[...]), _dot(cq, wpb_ref[...]), tka_ref[...], tkb_ref[...])
    qpe_ref[...] = (qpe * ATTN_SCALE).astype(BF16)
    zk = _dot(xn, w1_ref[:, C_KV:W1_COLS])
    lat = _rms(zk[:, 0:KV_RANK], kvn_ref[...])
    kr = zk[:, KV_RANK:KV_RANK + LANES] * tka_ref[...] + zk[:, KV_RANK + LANES:] * tkb_ref[...]
    lat_ref[...] = lat
    kr_ref[...] = kr[:, 0:QK_ROPE]
    krp_ref[...] = kr

    zd = _dot(xn, w1_ref[:, C_DV:C_KV])
    glu = zd[:, 0:BRANCH_W] * jax.nn.sigmoid(zd[:, BRANCH_W:2 * BRANCH_W])
    glu_ref[...] = glu
    full_d = [std_ref[i] for i in range(CF_WIDTH - 1)] + [blk(glu, t) for t in range(ts)]
    for t in range(ts):
        conv = cfb_ref[...] + cfw_ref[0:1, :] * full_d[t]
        for kk in range(1, CF_WIDTH):
            conv = conv + cfw_ref[kk:kk + 1, :] * full_d[t + kk]
        yabd_ref[t * ns:(t + 1) * ns, 2 * BRANCH_W:3 * BRANCH_W] = _layer_norm_silu(
            conv, lng_ref[...], lnb_ref[...]).astype(BF16)


def _mix_in_sample(h, ns, ts, lw, tabs, sta_t, std_t):
    rows = ns * ts
    hp = MLA_HEADS * HEAD_PAD
    args = (h, lw["mix_gain"], lw["w1"], lw["sc_w"], lw["gm_gain"], lw["gm_coef"], lw["gm_bias_rows"],
            lw["q_gain"], lw["kv_gain"], lw["wqa"], lw["wqb"], lw["wpa"], lw["wpb"], lw["wukt"],
            lw["cf_w"], lw["cf_b"], lw["ln_g"], lw["ln_b"],
            tabs["tqa"], tabs["tqb"], tabs["tka"], tabs["tkb"], sta_t, std_t)
    out_shape = [jax.ShapeDtypeStruct((rows, hp), BF16), jax.ShapeDtypeStruct((rows, hp), BF16),
                 jax.ShapeDtypeStruct((rows, 3 * BRANCH_W), BF16),
                 jax.ShapeDtypeStruct((rows, KV_RANK), F32), jax.ShapeDtypeStruct((rows, QK_ROPE), F32),
                 jax.ShapeDtypeStruct((rows, LANES), F32),
                 jax.ShapeDtypeStruct((SC_WIDTH - 1, ns, BRANCH_W), F32),
                 jax.ShapeDtypeStruct((rows, BRANCH_W), F32), jax.ShapeDtypeStruct((rows, BRANCH_W), F32)]

    def full(a):
        nd = a.ndim
        return pl.BlockSpec(a.shape, lambda i: (0,) * nd)

    return pl.pallas_call(
        functools.partial(_mix_in_sample_kernel, ns=ns, ts=ts),
        grid=(1,),
        in_specs=[full(a) for a in args],
        out_specs=[full(s) for s in out_shape],
        out_shape=out_shape,
        compiler_params=_params(("arbitrary",)),
        name="mix_in_sample",
    )(*args)


def _prompt_attn_kernel(qi_ref, ki_ref, q_ref, k_ref, v_ref, o_ref, m_s, l_s, acc_s, *, tile):
    step = pl.program_id(1)
    qi = qi_ref[step]
    ki = ki_ref[step]

    @pl.when(ki == 0)
    def _():
        m_s[...] = jnp.full(m_s.shape, -jnp.inf, F32)
        l_s[...] = jnp.zeros(l_s.shape, F32)
        acc_s[...] = jnp.zeros(acc_s.shape, F32)

    def update(masked):
        if masked:
            keep = (lax.broadcasted_iota(jnp.int32, (tile, tile), 1)
                    <= lax.broadcasted_iota(jnp.int32, (tile, tile), 0))
        for hd in range(MLA_HEADS):
            cols = slice(hd * HEAD_PAD, (hd + 1) * HEAD_PAD)
            s = _dot_nt(q_ref[:, cols], k_ref[:, cols])
            if masked:
                s = jnp.where(keep, s, -jnp.inf)
            m_prev = m_s[hd]
            m_new = jnp.maximum(m_prev, jnp.max(s, axis=1, keepdims=True))
            alpha = jnp.exp(m_prev - m_new)
            p = jnp.exp(s - m_new)
            l_s[hd] = alpha * l_s[hd] + jnp.sum(p, axis=1, keepdims=True)
            acc_s[hd] = alpha * acc_s[hd] + _dot(p.astype(BF16), v_ref[:, cols])
            m_s[hd] = m_new

    @pl.when(ki < qi)
    def _():
        update(False)

    @pl.when(ki == qi)
    def _():
        update(True)
        for hd in range(MLA_HEADS):
            o_ref[:, hd * HEAD_PAD:(hd + 1) * HEAD_PAD] = (acc_s[hd] / l_s[hd]).astype(BF16)


def _prompt_attn(q, k, v, n_seq, seq):
    tile = min(ATTN_TILE, seq)
    nt = seq // tile
    hp = MLA_HEADS * HEAD_PAD
    pairs = [(a, b) for a in range(nt) for b in range(a + 1)]
    qi_tab = jnp.asarray([p[0] for p in pairs], jnp.int32)
    ki_tab = jnp.asarray([p[1] for p in pairs], jnp.int32)
    q_spec = pl.BlockSpec((tile, hp), lambda b, s, qi, ki: (b * nt + qi[s], 0))
    k_spec = pl.BlockSpec((tile, hp), lambda b, s, qi, ki: (b * nt + ki[s], 0))
    return pl.pallas_call(
        functools.partial(_prompt_attn_kernel, tile=tile),
        grid_spec=pltpu.PrefetchScalarGridSpec(
            num_scalar_prefetch=2,
            grid=(n_seq, len(pairs)),
            in_specs=[q_spec, k_spec, k_spec],
            out_specs=q_spec,
            scratch_shapes=[pltpu.VMEM((MLA_HEADS, tile, 1), F32), pltpu.VMEM((MLA_HEADS, tile, 1), F32),
                            pltpu.VMEM((MLA_HEADS, tile, HEAD_PAD), F32)]),
        out_shape=jax.ShapeDtypeStruct((n_seq * seq, hp), BF16),
        compiler_params=_params(("arbitrary", "arbitrary")),
        name="prompt_attn",
    )(qi_tab, ki_tab, q, k, v)


def _decode_attn_kernel(pt_ref, qlat_ref, qpe_ref, latn_ref, krn_ref, wuv_ref, *rest, pages, ts):
    lat_refs = rest[:pages]
    kr_refs = rest[pages:2 * pages]
    o_ref, m_s, l_s, acc_s = rest[2 * pages:]
    j = pl.program_id(1)
    rows = ts * MLA_HEADS

    @pl.when(j == 0)
    def _():
        m_s[...] = jnp.full(m_s.shape, -jnp.inf, F32)
        l_s[...] = jnp.zeros(l_s.shape, F32)
        acc_s[...] = jnp.zeros(acc_s.shape, F32)

    ql = qlat_ref[0]
    qp = qpe_ref[0]

    def online(s, values):
        m_prev = m_s[...]
        m_new = jnp.maximum(m_prev, jnp.max(s, axis=1, keepdims=True))
        alpha = jnp.exp(m_prev - m_new)
        p = jnp.exp(s - m_new)
        l_s[...] = alpha * l_s[...] + jnp.sum(p, axis=1, keepdims=True)
        pv = jnp.zeros((rows, KV_RANK), F32)
        c0 = 0
        for val in values:
            n = val.shape[0]
            pv = pv + _dot(p[:, c0:c0 + n].astype(BF16), val)
            c0 += n
        acc_s[...] = alpha * acc_s[...] + pv
        m_s[...] = m_new

    lats = [r[...].astype(BF16) for r in lat_refs]
    scores = [_dot_nt(ql, lats[i]) + _dot_nt(qp[:, 0:QK_ROPE], kr_refs[i][...].astype(BF16))
              for i in range(pages)]
    online(jnp.concatenate(scores, axis=1), lats)

    @pl.when(j == pl.num_programs(1) - 1)
    def _():
        latn = latn_ref[0].astype(BF16)
        krn = krn_ref[0].astype(BF16)
        s = _dot_nt(ql, latn) + _dot_nt(qp, krn)
        n_new = latn.shape[0]
        t_of_row = lax.broadcasted_iota(jnp.int32, (rows, n_new), 0) // MLA_HEADS
        col = lax.broadcasted_iota(jnp.int32, (rows, n_new), 1)
        online(jnp.where(col <= t_of_row, s, -jnp.inf), [latn])
        o_lat = (acc_s[...] / l_s[...]).astype(BF16)
        head_of_row = lax.broadcasted_iota(jnp.int32, (rows, HEAD_PAD), 0) % MLA_HEADS
        out = jnp.zeros((rows, HEAD_PAD), F32)
        for hd in range(MLA_HEADS):
            out = jnp.where(head_of_row == hd, _dot(o_lat, wuv_ref[hd]), out)
        o_ref[0] = out.astype(BF16)


def _decode_attn(layer, page_table, qlat, qpe, latn, krn, wuv, cache_lat, cache_kr, ts):
    ns, n_pages = page_table.shape
    pages = min(PAGES_PER_STEP, n_pages)
    nj = n_pages // pages
    rows = ts * MLA_HEADS
    n_new = latn.shape[1]

    def seq_spec(r, w):
        return pl.BlockSpec((1, r, w), lambda n, j, pt: (n, 0, 0))

    def page_spec(w, i):
        return pl.BlockSpec((None, None, PAGE_SIZE, w), lambda n, j, pt: (layer, pt[n, j * pages + i], 0, 0))

    in_specs = ([seq_spec(rows, KV_RANK), seq_spec(rows, LANES), seq_spec(n_new, KV_RANK), seq_spec(n_new, LANES),
                 pl.BlockSpec((MLA_HEADS, KV_RANK, HEAD_PAD), lambda n, j, pt: (0, 0, 0))]
                + [page_spec(KV_RANK, i) for i in range(pages)]
                + [page_spec(QK_ROPE, i) for i in range(pages)])
    return pl.pallas_call(
        functools.partial(_decode_attn_kernel, pages=pages, ts=ts),
        grid_spec=pltpu.PrefetchScalarGridSpec(
            num_scalar_prefetch=1,
            grid=(ns, nj),
            in_specs=in_specs,
            out_specs=seq_spec(rows, HEAD_PAD),
            scratch_shapes=[pltpu.VMEM((rows, 1), F32), pltpu.VMEM((rows, 1), F32),
                            pltpu.VMEM((rows, KV_RANK), F32)]),
        out_shape=jax.ShapeDtypeStruct((ns, rows, HEAD_PAD), BF16),
        compiler_params=_params(("arbitrary", "arbitrary")),
        name="decode_attn",
    )(page_table, qlat, qpe, latn, krn, wuv, *([cache_lat] * pages), *([cache_kr] * pages))


def _pad_cols(x, width):
    return jnp.pad(x, ((0, 0), (0, width - x.shape[1])))


def _rot_cols(w):
    half = w.shape[-1] // 2
    return jnp.concatenate([-w[..., half:], w[..., :half]], axis=-1)


def _per_head(blocks):
    return jnp.concatenate([_pad_cols(b, HEAD_PAD) for b in blocks], axis=1)


def _pack_layer(l, p):
    w_in = p["w_in"][l]
    pts = [0]
    for s in (BRANCH_W,) * 5 + (Q_RANK, KV_RANK, QK_ROPE, BRANCH_W, BRANCH_W):
        pts.append(pts[-1] + s)
    a_h, a_c, a_b, b_u, b_v, c_q, c_kv, c_kr, d_v, d_g = (w_in[:, pts[i]:pts[i + 1]] for i in range(10))
    w1 = jnp.concatenate([a_h, a_c, a_b, b_u, b_v, c_q, d_v, d_g, c_kv,
                          _pad_cols(c_kr, LANES), _pad_cols(_rot_cols(c_kr), LANES)], axis=1).astype(BF16)
    w_uq = p["mla_w_uq"][l]
    w_uk = p["mla_w_uk"][l]
    w_uv = p["mla_w_uv"][l]
    zeros_nope = jnp.zeros((Q_RANK, QK_NOPE), F32)
    wqa = _per_head([w_uq[:, h, :] for h in range(MLA_HEADS)])
    wqb = _per_head([jnp.concatenate([zeros_nope, _rot_cols(w_uq[:, h, QK_NOPE:])], axis=1)
                     for h in range(MLA_HEADS)])
    wpa = _per_head([w_uq[:, h, QK_NOPE:] for h in range(MLA_HEADS)])
    wpb = _per_head([_rot_cols(w_uq[:, h, QK_NOPE:]) for h in range(MLA_HEADS)])
    wk = _per_head([w_uk[:, h, :] for h in range(MLA_HEADS)])
    wv = _per_head([w_uv[:, h, :] for h in range(MLA_HEADS)])
    place = jnp.zeros((LANES, HEAD_PAD), F32).at[jnp.arange(QK_ROPE), QK_NOPE + jnp.arange(QK_ROPE)].set(1.0)
    wkv = jnp.concatenate([jnp.concatenate([wk, wv], axis=1),
                           jnp.concatenate([jnp.tile(place, (1, MLA_HEADS)),
                                            jnp.zeros((LANES, MLA_HEADS * HEAD_PAD), F32)], axis=1)], axis=0)
    wukt = jnp.stack([jnp.pad(w_uk[:, h, :].T, ((0, HEAD_PAD - QK_NOPE), (0, 0))) for h in range(MLA_HEADS)])
    wuv = jnp.stack([_pad_cols(w_uv[:, h, :], HEAD_PAD) for h in range(MLA_HEADS)])
    w_branch = p["w_branch"][l]
    wbc = jnp.concatenate([jnp.pad(w_branch[2, h * V_HEAD:(h + 1) * V_HEAD, :], ((0, HEAD_PAD - V_HEAD), (0, 0)))
                           for h in range(MLA_HEADS)], axis=0)
    gm_w = p["gm_spatial_w"][l]
    gm_b = p["gm_spatial_b"][l]
    return dict(
        f1=(p["ffn1_norm"][l][None, :], p["ffn1_w_gate"][l].astype(BF16), p["ffn1_w_up"][l].astype(BF16),
            p["ffn1_w_down"][l].astype(BF16)),
        f2=(p["ffn2_norm"][l][None, :], p["ffn2_w_gate"][l].astype(BF16), p["ffn2_w_up"][l].astype(BF16),
            p["ffn2_w_down"][l].astype(BF16)),
        mix_gain=p["mix_norm"][l][None, :], w1=w1, wgate=w_in[:, pts[10]:].astype(BF16),
        sc_w=p["sc_conv_w"][l], gm_gain=p["gm_norm"][l][None, :], gm_w=gm_w,
        gm_bias_full=jnp.repeat(gm_b.T, GM_GROUP_W, axis=1),
        q_gain=p["mla_q_norm"][l][None, :], kv_gain=p["mla_kv_norm"][l][None, :],
        wqa=wqa.astype(BF16), wqb=wqb.astype(BF16), wpa=wpa.astype(BF16), wpb=wpb.astype(BF16),
        wkv=wkv.astype(BF16), wukt=wukt.astype(BF16), wuv=wuv.astype(BF16),
        cf_w=p["cf_dw_w"][l], cf_b=p["cf_dw_b"][l][None, :], ln_g=p["cf_ln_g"][l][None, :],
        ln_b=p["cf_ln_b"][l][None, :],
        wba=w_branch[0].astype(BF16), wbb=w_branch[1].astype(BF16), wbc=wbc.astype(BF16),
        wbd=w_branch[3].astype(BF16), wout=p["w_out"][l].astype(BF16))


def _sample_gm(lw, ts):
    gm_w = lw["gm_w"][:, :ts, :ts]
    coef = jnp.repeat(gm_w.transpose(1, 2, 0).reshape(ts * ts, GM_GROUPS), GM_GROUP_W, axis=1)
    bias = lw["gm_bias_full"][:ts]
    return coef, bias


def _rope_tables(pos):
    half = QK_ROPE // 2
    inv = ROPE_THETA ** (-jnp.arange(half, dtype=F32) / half)
    ang = pos.astype(F32)[:, None] * inv[None, :]
    cos2 = jnp.tile(jnp.cos(ang), (1, 2))
    sin2 = jnp.tile(jnp.sin(ang), (1, 2))
    n = pos.shape[0]
    ones = jnp.ones((n, QK_NOPE), F32)
    zeros = jnp.zeros((n, QK_NOPE), F32)
    return dict(tqa=_pad_cols(jnp.concatenate([ones, cos2], axis=1) * ATTN_SCALE, LANES),
                tqb=_pad_cols(jnp.concatenate([zeros, sin2], axis=1) * ATTN_SCALE, LANES),
                tka=_pad_cols(cos2, LANES), tkb=_pad_cols(sin2, LANES))


def kernel(x_prompt, x_sample, cache_mla_latent, cache_mla_krope, page_table, state_conv_a, state_conv_d,
           ffn1_norm, ffn1_w_gate, ffn1_w_up, ffn1_w_down, mix_norm, w_in, sc_conv_w, gm_norm,
           gm_spatial_w, gm_spatial_b, mla_q_norm, mla_kv_norm, mla_w_uq, mla_w_uk, mla_w_uv,
           cf_dw_w, cf_dw_b, cf_ln_g, cf_ln_b, w_branch, w_out, ffn2_norm, ffn2_w_gate, ffn2_w_up,
           ffn2_w_down, final_norm):
    p = dict(ffn1_norm=ffn1_norm, ffn1_w_gate=ffn1_w_gate, ffn1_w_up=ffn1_w_up, ffn1_w_down=ffn1_w_down,
             mix_norm=mix_norm, w_in=w_in, sc_conv_w=sc_conv_w, gm_norm=gm_norm, gm_spatial_w=gm_spatial_w,
             gm_spatial_b=gm_spatial_b, mla_q_norm=mla_q_norm, mla_kv_norm=mla_kv_norm, mla_w_uq=mla_w_uq,
             mla_w_uk=mla_w_uk, mla_w_uv=mla_w_uv, cf_dw_w=cf_dw_w, cf_dw_b=cf_dw_b, cf_ln_g=cf_ln_g,
             cf_ln_b=cf_ln_b, w_branch=w_branch, w_out=w_out, ffn2_norm=ffn2_norm, ffn2_w_gate=ffn2_w_gate,
             ffn2_w_up=ffn2_w_up, ffn2_w_down=ffn2_w_down)
    depth = w_in.shape[0]
    n_p, t_p, _ = x_prompt.shape
    n_s, t_s, _ = x_sample.shape
    past = page_table.shape[1] * PAGE_SIZE
    hp = MLA_HEADS * HEAD_PAD
    new_pad = -(-t_s // 8) * 8

    tabs_p = _rope_tables(jnp.arange(t_p, dtype=jnp.int32))
    tabs_s = {k: jnp.repeat(v, n_s, axis=0)
              for k, v in _rope_tables(past + jnp.arange(t_s, dtype=jnp.int32)).items()}
    final_gain = final_norm[None, :]

    h_p = x_prompt.reshape(n_p * t_p, D_MODEL)
    h_s = x_sample.transpose(1, 0, 2).reshape(t_s * n_s, D_MODEL)

    def to_seq_major(x):
        return x.reshape(t_s, n_s, x.shape[-1]).transpose(1, 0, 2)

    outs = {k: [] for k in ("lat_p", "kr_p", "ca_p", "cd_p", "lat_s", "kr_s", "ca_s", "cd_s", "v_s")}
    for l in range(depth):
        lw = _pack_layer(l, p)
        lw["gm_coef"], lw["gm_bias_rows"] = _sample_gm(lw, t_s)
        last = l == depth - 1

        h_p = _ffn(h_p, *lw["f1"])
        q, k, v, yabd, lat, kr, sta, std = _mix_in_prompt(h_p, n_p, t_p, lw, tabs_p)
        yc = _prompt_attn(q, k, v, n_p, t_p)
        h_p = _merge(h_p, yabd, yc, lw["mix_gain"], lw["wgate"], lw["wba"], lw["wbb"], lw["wbc"], lw["wbd"],
                     lw["wout"])
        h_p = _ffn(h_p, *lw["f2"], final_gain=final_gain if last else None)
        outs["lat_p"].append(lat.reshape(n_p, t_p, KV_RANK))
        outs["kr_p"].append(kr.reshape(n_p, t_p, QK_ROPE))
        outs["ca_p"].append(sta)
        outs["cd_p"].append(std)

        h_s = _ffn(h_s, *lw["f1"])
        sta_t = state_conv_a[l].transpose(1, 0, 2)
        std_t = state_conv_d[l].transpose(1, 0, 2)
        qlat, qpe, yabd_s, lat_s, kr_s, krp_s, nsta, glu, v_s = _mix_in_sample(h_s, n_s, t_s, lw, tabs_s, sta_t, std_t)

        def per_seq_rows(x):
            return x.reshape(t_s, n_s, MLA_HEADS, HEAD_PAD).transpose(1, 0, 2, 3).reshape(n_s, t_s * MLA_HEADS, HEAD_PAD)

        def new_rows(x):
            return jnp.pad(to_seq_major(x), ((0, 0), (0, new_pad - t_s), (0, 0)))

        yc_s = _decode_attn(l, page_table, per_seq_rows(qlat), per_seq_rows(qpe), new_rows(lat_s), new_rows(krp_s),
                            lw["wuv"], cache_mla_latent, cache_mla_krope, t_s)
        yc_s = yc_s.reshape(n_s, t_s, MLA_HEADS, HEAD_PAD).transpose(1, 0, 2, 3).reshape(t_s * n_s, hp)
        h_s = _merge(h_s, yabd_s, yc_s, lw["mix_gain"], lw["wgate"], lw["wba"], lw["wbb"], lw["wbc"], lw["wbd"],
                     lw["wout"])
        h_s = _ffn(h_s, *lw["f2"], final_gain=final_gain if last else None)
        outs["lat_s"].append(to_seq_major(lat_s))
        outs["kr_s"].append(to_seq_major(kr_s))
        outs["ca_s"].append(nsta.transpose(1, 0, 2))
        outs["cd_s"].append(jnp.concatenate([state_conv_d[l][:, t_s:, :], to_seq_major(glu)], axis=1))
        outs["v_s"].append(to_seq_major(v_s))

    y_prompt = h_p.reshape(n_p, t_p, D_MODEL)
    y_sample = to_seq_major(h_s)
    return (y_prompt, y_sample,
            jnp.stack(outs["lat_p"]), jnp.stack(outs["kr_p"]), jnp.stack(outs["ca_p"]), jnp.stack(outs["cd_p"]),
            jnp.stack(outs["lat_s"]), jnp.stack(outs["kr_s"]), jnp.stack(outs["ca_s"]), jnp.stack(outs["cd_s"]),
            jnp.stack(outs["v_s"]))
```

```python
import functools

import jax
import jax.numpy as jnp
from jax import lax
from jax.experimental import pallas as pl
from jax.experimental.pallas import tpu as pltpu

F32 = jnp.float32
BF16 = jnp.bfloat16

D_MODEL = 1024
BRANCH_W = D_MODEL // 4
SC_WIDTH = 3
GM_CHUNK = 128
GM_GROUPS = 4
GM_GROUP_W = BRANCH_W // GM_GROUPS
MLA_HEADS = 4
QK_NOPE = BRANCH_W // MLA_HEADS
QK_ROPE = QK_NOPE // 2
V_HEAD = BRANCH_W // MLA_HEADS
Q_RANK = D_MODEL // 4
KV_RANK = D_MODEL // 8
ROPE_THETA = 10000.0
CF_WIDTH = 31
D_FF = ((8 * D_MODEL // 3 + 255) // 256) * 256
PAGE_SIZE = 128
RMS_EPS = 1e-6
LN_EPS = 1e-5
ATTN_SCALE = (QK_NOPE + QK_ROPE) ** -0.5

LANES = 128
HEAD_PAD = LANES
TOKEN_TILE = 512
FF_CHUNK = 256
ATTN_TILE = 512
HALO_A = 8
HALO_D = 32
VMEM_LIMIT = 52 * 1024 * 1024

C_AH, C_AC, C_AB, C_BU, C_BV, C_CQ, C_DV, C_DG = (i * BRANCH_W for i in range(8))
C_KV = 8 * BRANCH_W
C_KX = C_KV + KV_RANK
C_KY = C_KX + LANES
W1_COLS = C_KY + LANES


def _rms(x, g):
    return x * lax.rsqrt(jnp.mean(x * x, axis=-1, keepdims=True) + RMS_EPS) * g


def _dot(a, b):
    return jnp.dot(a, b, preferred_element_type=F32)


def _dot_nt(a, b):
    return lax.dot_general(a, b, (((1,), (1,)), ((), ())), preferred_element_type=F32)


def _const_spec(shape):
    nd = len(shape)
    return pl.BlockSpec(shape, lambda *_: (0,) * nd, pipeline_mode=pl.Buffered(1))


def _params(sem):
    return pltpu.CompilerParams(dimension_semantics=sem, vmem_limit_bytes=VMEM_LIMIT)


def _ffn_kernel(*refs, final):
    if final:
        h_ref, g_ref, wg_ref, wu_ref, wd_ref, fg_ref, o_ref = refs
    else:
        h_ref, g_ref, wg_ref, wu_ref, wd_ref, o_ref = refs
    h = h_ref[...]
    xn = _rms(h, g_ref[...]).astype(BF16)
    acc = jnp.zeros_like(h)
    for c in range(0, D_FF, FF_CHUNK):
        gate = _dot(xn, wg_ref[:, c:c + FF_CHUNK])
        up = _dot(xn, wu_ref[:, c:c + FF_CHUNK])
        act = (jax.nn.silu(gate) * up).astype(BF16)
        acc = acc + _dot(act, wd_ref[c:c + FF_CHUNK, :])
    out = h + 0.5 * acc
    if final:
        out = _rms(out, fg_ref[...])
    o_ref[...] = out


def _ffn(h, gain, wg, wu, wd, final_gain=None):
    rows = h.shape[0]
    tm = min(TOKEN_TILE, rows)
    final = final_gain is not None
    row_spec = pl.BlockSpec((tm, D_MODEL), lambda i: (i, 0))
    in_specs = [row_spec, _const_spec((1, D_MODEL)), _const_spec((D_MODEL, D_FF)),
                _const_spec((D_MODEL, D_FF)), _const_spec((D_FF, D_MODEL))]
    args = [h, gain, wg, wu, wd]
    if final:
        in_specs.append(_const_spec((1, D_MODEL)))
        args.append(final_gain)
    return pl.pallas_call(
        functools.partial(_ffn_kernel, final=final),
        grid=(rows // tm,),
        in_specs=in_specs,
        out_specs=row_spec,
        out_shape=jax.ShapeDtypeStruct((rows, D_MODEL), F32),
        compiler_params=_params(("arbitrary",)),
        name="ffn_final" if final else "ffn",
    )(*args)


def _merge_kernel(h_ref, yabd_ref, yc_ref, g_ref, wgate_ref, wba_ref, wbb_ref, wbc_ref, wbd_ref,
                  wout_ref, o_ref):
    h = h_ref[...]
    xn = _rms(h, g_ref[...]).astype(BF16)
    ys = (yabd_ref[:, 0:BRANCH_W], yabd_ref[:, BRANCH_W:2 * BRANCH_W], yc_ref[...],
          yabd_ref[:, 2 * BRANCH_W:3 * BRANCH_W])
    wbs = (wba_ref, wbb_ref, wbc_ref, wbd_ref)
    merged = jnp.zeros_like(h)
    for i in range(4):
        logits = _dot(xn, wgate_ref[:, i * D_MODEL:(i + 1) * D_MODEL])
        merged = merged + jax.nn.sigmoid(logits) * _dot(ys[i], wbs[i][...])
    o_ref[...] = h + _dot(merged.astype(BF16), wout_ref[...])


def _merge(h, yabd, yc, gain, wgate, wba, wbb, wbc, wbd, wout):
    rows = h.shape[0]
    tm = min(TOKEN_TILE, rows)
    yc_w = MLA_HEADS * HEAD_PAD
    row_spec = pl.BlockSpec((tm, D_MODEL), lambda i: (i, 0))
    return pl.pallas_call(
        _merge_kernel,
        grid=(rows // tm,),
        in_specs=[row_spec,
                  pl.BlockSpec((tm, 3 * BRANCH_W), lambda i: (i, 0)),
                  pl.BlockSpec((tm, yc_w), lambda i: (i, 0)),
                  _const_spec((1, D_MODEL)),
                  _const_spec((D_MODEL, 4 * D_MODEL)),
                  _const_spec((BRANCH_W, D_MODEL)), _const_spec((BRANCH_W, D_MODEL)),
                  _const_spec((yc_w, D_MODEL)), _const_spec((BRANCH_W, D_MODEL)),
                  _const_spec((D_MODEL, D_MODEL))],
        out_specs=row_spec,
        out_shape=jax.ShapeDtypeStruct((rows, D_MODEL), F32),
        compiler_params=_params(("arbitrary",)),
        name="merge",
    )(h, yabd, yc, gain, wgate, wba, wbb, wbc, wbd, wout)


def _layer_norm_silu(x, g, b):
    mu = jnp.mean(x, axis=-1, keepdims=True)
    xc = x - mu
    var = jnp.mean(xc * xc, axis=-1, keepdims=True)
    return jax.nn.silu(xc * lax.rsqrt(var + LN_EPS) * g + b)


def _rope_heads(qa, qb, ta, tb):
    return jnp.concatenate(
        [qa[:, hd * HEAD_PAD:(hd + 1) * HEAD_PAD] * ta + qb[:, hd * HEAD_PAD:(hd + 1) * HEAD_PAD] * tb
         for hd in range(MLA_HEADS)], axis=1)


def _mix_in_prompt_kernel(h_ref, g_ref, w1_ref, scw_ref, gmn_ref, gmw_ref, gmb_ref, qn_ref, kvn_ref,
                          wqa_ref, wqb_ref, wkv_ref, cfw_ref, cfb_ref, lng_ref, lnb_ref,
                          tqa_ref, tqb_ref, tka_ref, tkb_ref,
                          q_ref, k_ref, v_ref, yabd_ref, lat_ref, kr_ref, sta_ref, std_ref,
                          bufa, bufd, *, tm):
    @pl.when(pl.program_id(1) == 0)
    def _():
        bufa[0:HALO_A, :] = jnp.zeros((HALO_A, BRANCH_W), F32)
        bufd[0:HALO_D, :] = jnp.zeros((HALO_D, BRANCH_W), F32)

    xn = _rms(h_ref[...], g_ref[...]).astype(BF16)

    za = _dot(xn, w1_ref[:, C_AH:C_BU])
    ga = za[:, BRANCH_W:2 * BRANCH_W] * za[:, 0:BRANCH_W]
    bufa[HALO_A:HALO_A + tm, :] = ga
    conv_a = (scw_ref[0:1, :] * bufa[HALO_A - 2:HALO_A - 2 + tm, :]
              + scw_ref[1:2, :] * bufa[HALO_A - 1:HALO_A - 1 + tm, :]
              + scw_ref[2:3, :] * ga)
    yabd_ref[:, 0:BRANCH_W] = (za[:, 2 * BRANCH_W:3 * BRANCH_W] * conv_a).astype(BF16)
    sta_ref[0] = bufa[HALO_A + tm - (SC_WIDTH - 1):HALO_A + tm, :]
    bufa[0:HALO_A, :] = bufa[tm:tm + HALO_A, :]

    zb = _dot(xn, w1_ref[:, C_BU:C_CQ])
    u = jax.nn.gelu(zb[:, 0:BRANCH_W])
    v = _rms(jax.nn.gelu(zb[:, BRANCH_W:2 * BRANCH_W]), gmn_ref[...]).astype(BF16)
    tril = (lax.broadcasted_iota(jnp.int32, (GM_CHUNK, GM_CHUNK), 0)
            >= lax.broadcasted_iota(jnp.int32, (GM_CHUNK, GM_CHUNK), 1))
    wms = [jnp.where(tril, gmw_ref[g], 0.0).astype(BF16) for g in range(GM_GROUPS)]
    group = lax.broadcasted_iota(jnp.int32, (GM_CHUNK, BRANCH_W), 1) // GM_GROUP_W
    for c in range(tm // GM_CHUNK):
        rows = slice(c * GM_CHUNK, (c + 1) * GM_CHUNK)
        vc = v[rows, :]
        mix = gmb_ref[...]
        for g in range(GM_GROUPS):
            mix = mix + _dot(wms[g], jnp.where(group == g, vc, jnp.zeros_like(vc)))
        yabd_ref[rows, BRANCH_W:2 * BRANCH_W] = (u[rows, :] * mix).astype(BF16)

    zc = _dot(xn, w1_ref[:, C_CQ:C_DV])
    cq = _rms(zc, qn_ref[...]).astype(BF16)
    q = _rope_heads(_dot(cq, wqa_ref[...]), _dot(cq, wqb_ref[...]), tqa_ref[...], tqb_ref[...])
    q_ref[...] = q.astype(BF16)
    zk = _dot(xn, w1_ref[:, C_KV:W1_COLS])
    lat = _rms(zk[:, 0:KV_RANK], kvn_ref[...])
    kr = zk[:, KV_RANK:KV_RANK + LANES] * tka_ref[...] + zk[:, KV_RANK + LANES:] * tkb_ref[...]
    lat_ref[...] = lat
    kr_ref[...] = kr[:, 0:QK_ROPE]
    kv = _dot(jnp.concatenate([lat, kr], axis=1).astype(BF16), wkv_ref[...])
    k_ref[...] = kv[:, 0:MLA_HEADS * HEAD_PAD].astype(BF16)
    v_ref[...] = kv[:, MLA_HEADS * HEAD_PAD:].astype(BF16)

    zd = _dot(xn, w1_ref[:, C_DV:C_KV])
    glu = zd[:, 0:BRANCH_W] * jax.nn.sigmoid(zd[:, BRANCH_W:2 * BRANCH_W])
    bufd[HALO_D:HALO_D + tm, :] = glu
    off = HALO_D - (CF_WIDTH - 1)
    conv_d = cfb_ref[...] + cfw_ref[0:1, :] * bufd[off:off + tm, :]
    for kk in range(1, CF_WIDTH):
        conv_d = conv_d + cfw_ref[kk:kk + 1, :] * bufd[off + kk:off + kk + tm, :]
    yabd_ref[:, 2 * BRANCH_W:3 * BRANCH_W] = _layer_norm_silu(conv_d, lng_ref[...], lnb_ref[...]).astype(BF16)
    std_ref[0] = bufd[HALO_D + tm - (CF_WIDTH - 1):HALO_D + tm, :]
    bufd[0:HALO_D, :] = bufd[tm:tm + HALO_D, :]


def _mix_in_prompt(h, n_seq, seq, lw, tabs):
    tm = min(TOKEN_TILE, seq)
    nt = seq // tm
    rows = n_seq * seq
    hp = MLA_HEADS * HEAD_PAD

    def rowspec(w):
        return pl.BlockSpec((tm, w), lambda b, j: (b * nt + j, 0))

    def tabspec():
        return pl.BlockSpec((tm, LANES), lambda b, j: (j, 0))

    in_specs = [rowspec(D_MODEL), _const_spec((1, D_MODEL)), _const_spec((D_MODEL, W1_COLS)),
                _const_spec((SC_WIDTH, BRANCH_W)), _const_spec((1, BRANCH_W)),
                _const_spec((GM_GROUPS, GM_CHUNK, GM_CHUNK)), _const_spec((GM_CHUNK, BRANCH_W)),
                _const_spec((1, Q_RANK)), _const_spec((1, KV_RANK)),
                _const_spec((Q_RANK, hp)), _const_spec((Q_RANK, hp)), _const_spec((KV_RANK + LANES, 2 * hp)),
                _const_spec((CF_WIDTH, BRANCH_W)), _const_spec((1, BRANCH_W)),
                _const_spec((1, BRANCH_W)), _const_spec((1, BRANCH_W)),
                tabspec(), tabspec(), tabspec(), tabspec()]
    out_specs = [rowspec(hp), rowspec(hp), rowspec(hp), rowspec(3 * BRANCH_W), rowspec(KV_RANK),
                 rowspec(QK_ROPE),
                 pl.BlockSpec((1, SC_WIDTH - 1, BRANCH_W), lambda b, j: (b, 0, 0)),
                 pl.BlockSpec((1, CF_WIDTH - 1, BRANCH_W), lambda b, j: (b, 0, 0))]
    out_shape = [jax.ShapeDtypeStruct((rows, hp), BF16), jax.ShapeDtypeStruct((rows, hp), BF16),
                 jax.ShapeDtypeStruct((rows, hp), BF16), jax.ShapeDtypeStruct((rows, 3 * BRANCH_W), BF16),
                 jax.ShapeDtypeStruct((rows, KV_RANK), F32), jax.ShapeDtypeStruct((rows, QK_ROPE), F32),
                 jax.ShapeDtypeStruct((n_seq, SC_WIDTH - 1, BRANCH_W), F32),
                 jax.ShapeDtypeStruct((n_seq, CF_WIDTH - 1, BRANCH_W), F32)]
    return pl.pallas_call(
        functools.partial(_mix_in_prompt_kernel, tm=tm),
        grid=(n_seq, nt),
        in_specs=in_specs,
        out_specs=out_specs,
        out_shape=out_shape,
        scratch_shapes=[pltpu.VMEM((HALO_A + tm, BRANCH_W), F32), pltpu.VMEM((HALO_D + tm, BRANCH_W), F32)],
        compiler_params=_params(("arbitrary", "arbitrary")),
        name="mix_in_prompt",
    )(h, lw["mix_gain"], lw["w1"], lw["sc_w"], lw["gm_gain"], lw["gm_w"], lw["gm_bias_full"],
      lw["q_gain"], lw["kv_gain"], lw["wqa"], lw["wqb"], lw["wkv"], lw["cf_w"], lw["cf_b"],
      lw["ln_g"], lw["ln_b"], tabs["tqa"], tabs["tqb"], tabs["tka"], tabs["tkb"])


def _mix_in_sample_kernel(h_ref, g_ref, w1_ref, scw_ref, gmn_ref, coef_ref, gmb_ref, qn_ref, kvn_ref,
                          wqa_ref, wqb_ref, wpa_ref, wpb_ref, wukt_ref, cfw_ref, cfb_ref, lng_ref, lnb_ref,
                          tqa_ref, tqb_ref, tka_ref, tkb_ref, sta_ref, std_ref,
                          qlat_ref, qpe_ref, yabd_ref, lat_ref, kr_ref, nsta_ref, glu_ref, v_ref,
                          *, ns, ts):
    def blk(x, t):
        return x[t * ns:(t + 1) * ns, :]

    xn = _rms(h_ref[...], g_ref[...]).astype(BF16)

    za = _dot(xn, w1_ref[:, C_AH:C_BU])
    ga = za[:, BRANCH_W:2 * BRANCH_W] * za[:, 0:BRANCH_W]
    full_a = [sta_ref[i] for i in range(SC_WIDTH - 1)] + [blk(ga, t) for t in range(ts)]
    for t in range(ts):
        conv = scw_ref[0:1, :] * full_a[t]
        for kk in range(1, SC_WIDTH):
            conv = conv + scw_ref[kk:kk + 1, :] * full_a[t + kk]
        yabd_ref[t * ns:(t + 1) * ns, 0:BRANCH_W] = (blk(za, t)[:, 2 * BRANCH_W:3 * BRANCH_W] * conv).astype(BF16)
    for i in range(SC_WIDTH - 1):
        nsta_ref[i] = full_a[ts + i]

    zb = _dot(xn, w1_ref[:, C_BU:C_CQ])
    u = jax.nn.gelu(zb[:, 0:BRANCH_W])
    v = _rms(jax.nn.gelu(zb[:, BRANCH_W:2 * BRANCH_W]), gmn_ref[...])
    v_ref[...] = v
    for i in range(ts):
        mix = gmb_ref[i:i + 1, :] + coef_ref[i * ts:i * ts + 1, :] * blk(v, 0)
        for j in range(1, i + 1):
            mix = mix + coef_ref[i * ts + j:i * ts + j + 1, :] * blk(v, j)
        yabd_ref[i * ns:(i + 1) * ns, BRANCH_W:2 * BRANCH_W] = (blk(u, i) * mix).astype(BF16)

    zc = _dot(xn, w1_ref[:, C_CQ:C_DV])
    cq = _rms(zc, qn_ref[...]).astype(BF16)
    q = _rope_heads(_dot(cq, wqa_ref[...]), _dot(cq, wqb_ref[...]), tqa_ref[...], tqb_ref[...]).astype(BF16)
    qlat_ref[...] = jnp.concatenate(
        [_dot(q[:, hd * HEAD_PAD:(hd + 1) * HEAD_PAD], wukt_ref[hd]) for hd in range(MLA_HEADS)],
        axis=1).astype(BF16)
    qpe = _rope_heads(_dot(cq, wpa_ref[...]), _dot(cq, wpb_ref[...]), tka_ref[...], tkb_ref[...])
    qpe_ref[...] = (qpe * ATTN_SCALE).astype(BF16)
    zk = _dot(xn, w1_ref[:, C_KV:W1_COLS])
    lat = _rms(zk[:, 0:KV_RANK], kvn_ref[...])
    kr = zk[:, KV_RANK:KV_RANK + LANES] * tka_ref[...] + zk[:, KV_RANK + LANES:] * tkb_ref[...]
    lat_ref[...] = lat
    kr_ref[...] = kr[:, 0:QK_ROPE]

    zd = _dot(xn, w1_ref[:, C_DV:C_KV])
    glu = zd[:, 0:BRANCH_W] * jax.nn.sigmoid(zd[:, BRANCH_W:2 * BRANCH_W])
    glu_ref[...] = glu
    full_d = [std_ref[i] for i in range(CF_WIDTH - 1)] + [blk(glu, t) for t in range(ts)]
    for t in range(ts):
        conv = cfb_ref[...] + cfw_ref[0:1, :] * full_d[t]
        for kk in range(1, CF_WIDTH):
            conv = conv + cfw_ref[kk:kk + 1, :] * full_d[t + kk]
        yabd_ref[t * ns:(t + 1) * ns, 2 * BRANCH_W:3 * BRANCH_W] = _layer_norm_silu(
            conv, lng_ref[...], lnb_ref[...]).astype(BF16)


def _mix_in_sample(h, ns, ts, lw, tabs, sta_t, std_t):
    rows = ns * ts
    hp = MLA_HEADS * HEAD_PAD
    args = (h, lw["mix_gain"], lw["w1"], lw["sc_w"], lw["gm_gain"], lw["gm_coef"], lw["gm_bias_rows"],
            lw["q_gain"], lw["kv_gain"], lw["wqa"], lw["wqb"], lw["wpa"], lw["wpb"], lw["wukt"],
            lw["cf_w"], lw["cf_b"], lw["ln_g"], lw["ln_b"],
            tabs["tqa"], tabs["tqb"], tabs["tka"], tabs["tkb"], sta_t, std_t)
    out_shape = [jax.ShapeDtypeStruct((rows, hp), BF16), jax.ShapeDtypeStruct((rows, hp), BF16),
                 jax.ShapeDtypeStruct((rows, 3 * BRANCH_W), BF16),
                 jax.ShapeDtypeStruct((rows, KV_RANK), F32), jax.ShapeDtypeStruct((rows, QK_ROPE), F32),
                 jax.ShapeDtypeStruct((SC_WIDTH - 1, ns, BRANCH_W), F32),
                 jax.ShapeDtypeStruct((rows, BRANCH_W), F32), jax.ShapeDtypeStruct((rows, BRANCH_W), F32)]

    def full(a):
        nd = a.ndim
        return pl.BlockSpec(a.shape, lambda i: (0,) * nd)

    return pl.pallas_call(
        functools.partial(_mix_in_sample_kernel, ns=ns, ts=ts),
        grid=(1,),
        in_specs=[full(a) for a in args],
        out_specs=[full(s) for s in out_shape],
        out_shape=out_shape,
        compiler_params=_params(("arbitrary",)),
        name="mix_in_sample",
    )(*args)


def _prompt_attn_kernel(qi_ref, ki_ref, q_ref, k_ref, v_ref, o_ref, m_s, l_s, acc_s, *, tile):
    step = pl.program_id(1)
    qi = qi_ref[step]
    ki = ki_ref[step]

    @pl.when(ki == 0)
    def _():
        m_s[...] = jnp.full(m_s.shape, -jnp.inf, F32)
        l_s[...] = jnp.zeros(l_s.shape, F32)
        acc_s[...] = jnp.zeros(acc_s.shape, F32)

    def update(masked):
        if masked:
            keep = (lax.broadcasted_iota(jnp.int32, (tile, tile), 1)
                    <= lax.broadcasted_iota(jnp.int32, (tile, tile), 0))
        for hd in range(MLA_HEADS):
            cols = slice(hd * HEAD_PAD, (hd + 1) * HEAD_PAD)
            s = _dot_nt(q_ref[:, cols], k_ref[:, cols])
            if masked:
                s = jnp.where(keep, s, -jnp.inf)
            m_prev = m_s[hd]
            m_new = jnp.maximum(m_prev, jnp.max(s, axis=1, keepdims=True))
            alpha = jnp.exp(m_prev - m_new)
            p = jnp.exp(s - m_new)
            l_s[hd] = alpha * l_s[hd] + jnp.sum(p, axis=1, keepdims=True)
            acc_s[hd] = alpha * acc_s[hd] + _dot(p.astype(BF16), v_ref[:, cols])
            m_s[hd] = m_new

    @pl.when(ki < qi)
    def _():
        update(False)

    @pl.when(ki == qi)
    def _():
        update(True)
        for hd in range(MLA_HEADS):
            o_ref[:, hd * HEAD_PAD:(hd + 1) * HEAD_PAD] = (acc_s[hd] / l_s[hd]).astype(BF16)


def _prompt_attn(q, k, v, n_seq, seq):
    tile = min(ATTN_TILE, seq)
    nt = seq // tile
    hp = MLA_HEADS * HEAD_PAD
    pairs = [(a, b) for a in range(nt) for b in range(a + 1)]
    qi_tab = jnp.asarray([p[0] for p in pairs], jnp.int32)
    ki_tab = jnp.asarray([p[1] for p in pairs], jnp.int32)
    q_spec = pl.BlockSpec((tile, hp), lambda b, s, qi, ki: (b * nt + qi[s], 0))
    k_spec = pl.BlockSpec((tile, hp), lambda b, s, qi, ki: (b * nt + ki[s], 0))
    return pl.pallas_call(
        functools.partial(_prompt_attn_kernel, tile=tile),
        grid_spec=pltpu.PrefetchScalarGridSpec(
            num_scalar_prefetch=2,
            grid=(n_seq, len(pairs)),
            in_specs=[q_spec, k_spec, k_spec],
            out_specs=q_spec,
            scratch_shapes=[pltpu.VMEM((MLA_HEADS, tile, 1), F32), pltpu.VMEM((MLA_HEADS, tile, 1), F32),
                            pltpu.VMEM((MLA_HEADS, tile, HEAD_PAD), F32)]),
        out_shape=jax.ShapeDtypeStruct((n_seq * seq, hp), BF16),
        compiler_params=_params(("arbitrary", "arbitrary")),
        name="prompt_attn",
    )(qi_tab, ki_tab, q, k, v)


def _decode_attn_kernel(pt_ref, qlat_ref, qpe_ref, latn_ref, krtn_ref, wuv_ref, *rest, pages, ts):
    lat_refs = rest[:pages]
    krt_refs = rest[pages:2 * pages]
    o_ref, lat_s, krt_s = rest[2 * pages:]
    rows = ts * MLA_HEADS
    past = pages * PAGE_SIZE

    for i in range(pages):
        lat_s[i * PAGE_SIZE:(i + 1) * PAGE_SIZE, :] = lat_refs[i][...].astype(BF16)
        krt_s[:, i * PAGE_SIZE:(i + 1) * PAGE_SIZE] = krt_refs[i][...].astype(BF16)
    n_new = latn_ref.shape[1]
    latn = jnp.concatenate([latn_ref[0], jnp.zeros((PAGE_SIZE - n_new, KV_RANK), F32)], axis=0)
    lat_s[past:past + PAGE_SIZE, :] = latn.astype(BF16)
    krt_s[:, past:past + PAGE_SIZE] = krtn_ref[0].astype(BF16)

    s = _dot_nt(qlat_ref[0], lat_s[...]) + _dot(qpe_ref[0][:, 0:QK_ROPE], krt_s[...])
    t_of_row = lax.broadcasted_iota(jnp.int32, s.shape, 0) // MLA_HEADS
    col = lax.broadcasted_iota(jnp.int32, s.shape, 1)
    s = jnp.where(col <= past + t_of_row, s, -jnp.inf)
    p = jnp.exp(s - jnp.max(s, axis=1, keepdims=True))
    o_lat = (_dot(p.astype(BF16), lat_s[...]) / jnp.sum(p, axis=1, keepdims=True)).astype(BF16)
    head_of_row = lax.broadcasted_iota(jnp.int32, (rows, HEAD_PAD), 0) % MLA_HEADS
    out = jnp.zeros((rows, HEAD_PAD), F32)
    for hd in range(MLA_HEADS):
        out = jnp.where(head_of_row == hd, _dot(o_lat, wuv_ref[hd]), out)
    o_ref[0] = out.astype(BF16)


def _decode_attn(layer, page_table, qlat, qpe, latn, krtn, wuv, cache_lat, cache_krt, ts):
    ns, pages = page_table.shape
    rows = ts * MLA_HEADS
    n_new = latn.shape[1]
    keys = (pages + 1) * PAGE_SIZE

    def seq_spec(r, w):
        return pl.BlockSpec((1, r, w), lambda n, pt: (n, 0, 0))

    def page_spec(r, w, i):
        return pl.BlockSpec((None, None, r, w), lambda n, pt: (layer, pt[n, i], 0, 0))

    in_specs = ([seq_spec(rows, KV_RANK), seq_spec(rows, LANES), seq_spec(n_new, KV_RANK),
                 seq_spec(QK_ROPE, PAGE_SIZE),
                 pl.BlockSpec((MLA_HEADS, KV_RANK, HEAD_PAD), lambda n, pt: (0, 0, 0))]
                + [page_spec(PAGE_SIZE, KV_RANK, i) for i in range(pages)]
                + [page_spec(QK_ROPE, PAGE_SIZE, i) for i in range(pages)])
    return pl.pallas_call(
        functools.partial(_decode_attn_kernel, pages=pages, ts=ts),
        grid_spec=pltpu.PrefetchScalarGridSpec(
            num_scalar_prefetch=1,
            grid=(ns,),
            in_specs=in_specs,
            out_specs=seq_spec(rows, HEAD_PAD),
            scratch_shapes=[pltpu.VMEM((keys, KV_RANK), BF16), pltpu.VMEM((QK_ROPE, keys), BF16)]),
        out_shape=jax.ShapeDtypeStruct((ns, rows, HEAD_PAD), BF16),
        compiler_params=_params(("arbitrary",)),
        name="decode_attn",
    )(page_table, qlat, qpe, latn, krtn, wuv, *([cache_lat] * pages), *([cache_krt] * pages))


def _pad_cols(x, width):
    return jnp.pad(x, ((0, 0), (0, width - x.shape[1])))


def _rot_cols(w):
    half = w.shape[-1] // 2
    return jnp.concatenate([-w[..., half:], w[..., :half]], axis=-1)


def _per_head(blocks):
    return jnp.concatenate([_pad_cols(b, HEAD_PAD) for b in blocks], axis=1)


def _pack_layer(l, p):
    w_in = p["w_in"][l]
    pts = [0]
    for s in (BRANCH_W,) * 5 + (Q_RANK, KV_RANK, QK_ROPE, BRANCH_W, BRANCH_W):
        pts.append(pts[-1] + s)
    a_h, a_c, a_b, b_u, b_v, c_q, c_kv, c_kr, d_v, d_g = (w_in[:, pts[i]:pts[i + 1]] for i in range(10))
    w1 = jnp.concatenate([a_h, a_c, a_b, b_u, b_v, c_q, d_v, d_g, c_kv,
                          _pad_cols(c_kr, LANES), _pad_cols(_rot_cols(c_kr), LANES)], axis=1).astype(BF16)
    w_uq = p["mla_w_uq"][l]
    w_uk = p["mla_w_uk"][l]
    w_uv = p["mla_w_uv"][l]
    zeros_nope = jnp.zeros((Q_RANK, QK_NOPE), F32)
    wqa = _per_head([w_uq[:, h, :] for h in range(MLA_HEADS)])
    wqb = _per_head([jnp.concatenate([zeros_nope, _rot_cols(w_uq[:, h, QK_NOPE:])], axis=1)
                     for h in range(MLA_HEADS)])
    wpa = _per_head([w_uq[:, h, QK_NOPE:] for h in range(MLA_HEADS)])
    wpb = _per_head([_rot_cols(w_uq[:, h, QK_NOPE:]) for h in range(MLA_HEADS)])
    wk = _per_head([w_uk[:, h, :] for h in range(MLA_HEADS)])
    wv = _per_head([w_uv[:, h, :] for h in range(MLA_HEADS)])
    place = jnp.zeros((LANES, HEAD_PAD), F32).at[jnp.arange(QK_ROPE), QK_NOPE + jnp.arange(QK_ROPE)].set(1.0)
    wkv = jnp.concatenate([jnp.concatenate([wk, wv], axis=1),
                           jnp.concatenate([jnp.tile(place, (1, MLA_HEADS)),
                                            jnp.zeros((LANES, MLA_HEADS * HEAD_PAD), F32)], axis=1)], axis=0)
    wukt = jnp.stack([jnp.pad(w_uk[:, h, :].T, ((0, HEAD_PAD - QK_NOPE), (0, 0))) for h in range(MLA_HEADS)])
    wuv = jnp.stack([_pad_cols(w_uv[:, h, :], HEAD_PAD) for h in range(MLA_HEADS)])
    w_branch = p["w_branch"][l]
    wbc = jnp.concatenate([jnp.pad(w_branch[2, h * V_HEAD:(h + 1) * V_HEAD, :], ((0, HEAD_PAD - V_HEAD), (0, 0)))
                           for h in range(MLA_HEADS)], axis=0)
    gm_w = p["gm_spatial_w"][l]
    gm_b = p["gm_spatial_b"][l]
    return dict(
        f1=(p["ffn1_norm"][l][None, :], p["ffn1_w_gate"][l].astype(BF16), p["ffn1_w_up"][l].astype(BF16),
            p["ffn1_w_down"][l].astype(BF16)),
        f2=(p["ffn2_norm"][l][None, :], p["ffn2_w_gate"][l].astype(BF16), p["ffn2_w_up"][l].astype(BF16),
            p["ffn2_w_down"][l].astype(BF16)),
        mix_gain=p["mix_norm"][l][None, :], w1=w1, wgate=w_in[:, pts[10]:].astype(BF16),
        sc_w=p["sc_conv_w"][l], gm_gain=p["gm_norm"][l][None, :], gm_w=gm_w,
        gm_bias_full=jnp.repeat(gm_b.T, GM_GROUP_W, axis=1),
        q_gain=p["mla_q_norm"][l][None, :], kv_gain=p["mla_kv_norm"][l][None, :],
        wqa=wqa.astype(BF16), wqb=wqb.astype(BF16), wpa=wpa.astype(BF16), wpb=wpb.astype(BF16),
        wkv=wkv.astype(BF16), wukt=wukt.astype(BF16), wuv=wuv.astype(BF16),
        cf_w=p["cf_dw_w"][l], cf_b=p["cf_dw_b"][l][None, :], ln_g=p["cf_ln_g"][l][None, :],
        ln_b=p["cf_ln_b"][l][None, :],
        wba=w_branch[0].astype(BF16), wbb=w_branch[1].astype(BF16), wbc=wbc.astype(BF16),
        wbd=w_branch[3].astype(BF16), wout=p["w_out"][l].astype(BF16))


def _sample_gm(lw, ts):
    gm_w = lw["gm_w"][:, :ts, :ts]
    coef = jnp.repeat(gm_w.transpose(1, 2, 0).reshape(ts * ts, GM_GROUPS), GM_GROUP_W, axis=1)
    bias = lw["gm_bias_full"][:ts]
    return coef, bias


def _rope_tables(pos):
    half = QK_ROPE // 2
    inv = ROPE_THETA ** (-jnp.arange(half, dtype=F32) / half)
    ang = pos.astype(F32)[:, None] * inv[None, :]
    cos2 = jnp.tile(jnp.cos(ang), (1, 2))
    sin2 = jnp.tile(jnp.sin(ang), (1, 2))
    n = pos.shape[0]
    ones = jnp.ones((n, QK_NOPE), F32)
    zeros = jnp.zeros((n, QK_NOPE), F32)
    return dict(tqa=_pad_cols(jnp.concatenate([ones, cos2], axis=1) * ATTN_SCALE, LANES),
                tqb=_pad_cols(jnp.concatenate([zeros, sin2], axis=1) * ATTN_SCALE, LANES),
                tka=_pad_cols(cos2, LANES), tkb=_pad_cols(sin2, LANES))


def kernel(x_prompt, x_sample, cache_mla_latent, cache_mla_krope, page_table, state_conv_a, state_conv_d,
           ffn1_norm, ffn1_w_gate, ffn1_w_up, ffn1_w_down, mix_norm, w_in, sc_conv_w, gm_norm,
           gm_spatial_w, gm_spatial_b, mla_q_norm, mla_kv_norm, mla_w_uq, mla_w_uk, mla_w_uv,
           cf_dw_w, cf_dw_b, cf_ln_g, cf_ln_b, w_branch, w_out, ffn2_norm, ffn2_w_gate, ffn2_w_up,
           ffn2_w_down, final_norm):
    p = dict(ffn1_norm=ffn1_norm, ffn1_w_gate=ffn1_w_gate, ffn1_w_up=ffn1_w_up, ffn1_w_down=ffn1_w_down,
             mix_norm=mix_norm, w_in=w_in, sc_conv_w=sc_conv_w, gm_norm=gm_norm, gm_spatial_w=gm_spatial_w,
             gm_spatial_b=gm_spatial_b, mla_q_norm=mla_q_norm, mla_kv_norm=mla_kv_norm, mla_w_uq=mla_w_uq,
             mla_w_uk=mla_w_uk, mla_w_uv=mla_w_uv, cf_dw_w=cf_dw_w, cf_dw_b=cf_dw_b, cf_ln_g=cf_ln_g,
             cf_ln_b=cf_ln_b, w_branch=w_branch, w_out=w_out, ffn2_norm=ffn2_norm, ffn2_w_gate=ffn2_w_gate,
             ffn2_w_up=ffn2_w_up, ffn2_w_down=ffn2_w_down)
    depth = w_in.shape[0]
    n_p, t_p, _ = x_prompt.shape
    n_s, t_s, _ = x_sample.shape
    past = page_table.shape[1] * PAGE_SIZE
    hp = MLA_HEADS * HEAD_PAD
    new_pad = -(-t_s // 8) * 8
    cache_krt = jnp.swapaxes(cache_mla_krope, 2, 3)

    tabs_p = _rope_tables(jnp.arange(t_p, dtype=jnp.int32))
    tabs_s = {k: jnp.repeat(v, n_s, axis=0)
              for k, v in _rope_tables(past + jnp.arange(t_s, dtype=jnp.int32)).items()}
    final_gain = final_norm[None, :]

    h_p = x_prompt.reshape(n_p * t_p, D_MODEL)
    h_s = x_sample.transpose(1, 0, 2).reshape(t_s * n_s, D_MODEL)

    def to_seq_major(x):
        return x.reshape(t_s, n_s, x.shape[-1]).transpose(1, 0, 2)

    outs = {k: [] for k in ("lat_p", "kr_p", "ca_p", "cd_p", "lat_s", "kr_s", "ca_s", "cd_s", "v_s")}
    for l in range(depth):
        lw = _pack_layer(l, p)
        lw["gm_coef"], lw["gm_bias_rows"] = _sample_gm(lw, t_s)
        last = l == depth - 1

        h_p = _ffn(h_p, *lw["f1"])
        q, k, v, yabd, lat, kr, sta, std = _mix_in_prompt(h_p, n_p, t_p, lw, tabs_p)
        yc = _prompt_attn(q, k, v, n_p, t_p)
        h_p = _merge(h_p, yabd, yc, lw["mix_gain"], lw["wgate"], lw["wba"], lw["wbb"], lw["wbc"], lw["wbd"],
                     lw["wout"])
        h_p = _ffn(h_p, *lw["f2"], final_gain=final_gain if last else None)
        outs["lat_p"].append(lat.reshape(n_p, t_p, KV_RANK))
        outs["kr_p"].append(kr.reshape(n_p, t_p, QK_ROPE))
        outs["ca_p"].append(sta)
        outs["cd_p"].append(std)

        h_s = _ffn(h_s, *lw["f1"])
        sta_t = state_conv_a[l].transpose(1, 0, 2)
        std_t = state_conv_d[l].transpose(1, 0, 2)
        qlat, qpe, yabd_s, lat_s, kr_s, nsta, glu, v_s = _mix_in_sample(h_s, n_s, t_s, lw, tabs_s, sta_t, std_t)

        def per_seq_rows(x):
            return x.reshape(t_s, n_s, MLA_HEADS, HEAD_PAD).transpose(1, 0, 2, 3).reshape(n_s, t_s * MLA_HEADS, HEAD_PAD)

        def new_rows(x):
            return jnp.pad(to_seq_major(x), ((0, 0), (0, new_pad - t_s), (0, 0)))

        krt_new = jnp.pad(to_seq_major(kr_s).transpose(0, 2, 1), ((0, 0), (0, 0), (0, PAGE_SIZE - t_s)))
        yc_s = _decode_attn(l, page_table, per_seq_rows(qlat), per_seq_rows(qpe), new_rows(lat_s), krt_new,
                            lw["wuv"], cache_mla_latent, cache_krt, t_s)
        yc_s = yc_s.reshape(n_s, t_s, MLA_HEADS, HEAD_PAD).transpose(1, 0, 2, 3).reshape(t_s * n_s, hp)
        h_s = _merge(h_s, yabd_s, yc_s, lw["mix_gain"], lw["wgate"], lw["wba"], lw["wbb"], lw["wbc"], lw["wbd"],
                     lw["wout"])
        h_s = _ffn(h_s, *lw["f2"], final_gain=final_gain if last else None)
        outs["lat_s"].append(to_seq_major(lat_s))
        outs["kr_s"].append(to_seq_major(kr_s))
        outs["ca_s"].append(nsta.transpose(1, 0, 2))
        outs["cd_s"].append(jnp.concatenate([state_conv_d[l][:, t_s:, :], to_seq_major(glu)], axis=1))
        outs["v_s"].append(to_seq_major(v_s))

    y_prompt = h_p.reshape(n_p, t_p, D_MODEL)
    y_sample = to_seq_major(h_s)
    return (y_prompt, y_sample,
            jnp.stack(outs["lat_p"]), jnp.stack(outs["kr_p"]), jnp.stack(outs["ca_p"]), jnp.stack(outs["cd_p"]),
            jnp.stack(outs["lat_s"]), jnp.stack(outs["kr_s"]), jnp.stack(outs["ca_s"]), jnp.stack(outs["cd_s"]),
            jnp.stack(outs["v_s"]))
```

```python
import functools

import jax
import jax.numpy as jnp
from jax import lax
from jax.experimental import pallas as pl
from jax.experimental.pallas import tpu as pltpu

F32 = jnp.float32
BF16 = jnp.bfloat16

D_MODEL = 1024
BRANCH_W = D_MODEL // 4
SC_WIDTH = 3
GM_CHUNK = 128
GM_GROUPS = 4
GM_GROUP_W = BRANCH_W // GM_GROUPS
MLA_HEADS = 4
QK_NOPE = BRANCH_W // MLA_HEADS
QK_ROPE = QK_NOPE // 2
V_HEAD = BRANCH_W // MLA_HEADS
Q_RANK = D_MODEL // 4
KV_RANK = D_MODEL // 8
ROPE_THETA = 10000.0
CF_WIDTH = 31
D_FF = ((8 * D_MODEL // 3 + 255) // 256) * 256
PAGE_SIZE = 128
RMS_EPS = 1e-6
LN_EPS = 1e-5
ATTN_SCALE = (QK_NOPE + QK_ROPE) ** -0.5
LOG2_E = 1.4426950408889634

LANES = 128
HEAD_PAD = LANES
TOKEN_TILE = 512
FF_CHUNK = 256
ATTN_TILE = 512
HALO_A = 8
HALO_D = 32
VMEM_LIMIT = 52 * 1024 * 1024

C_AH, C_AC, C_AB, C_BU, C_BV, C_CQ, C_DV, C_DG = (i * BRANCH_W for i in range(8))
C_KV = 8 * BRANCH_W
C_KX = C_KV + KV_RANK
C_KY = C_KX + LANES
W1_COLS = C_KY + LANES


def _rms(x, g):
    return x * lax.rsqrt(jnp.mean(x * x, axis=-1, keepdims=True) + RMS_EPS) * g


def _dot(a, b):
    return jnp.dot(a, b, preferred_element_type=F32)


def _dot_nt(a, b):
    return lax.dot_general(a, b, (((1,), (1,)), ((), ())), preferred_element_type=F32)


def _const_spec(shape):
    nd = len(shape)
    return pl.BlockSpec(shape, lambda *_: (0,) * nd, pipeline_mode=pl.Buffered(1))


def _params(sem):
    return pltpu.CompilerParams(dimension_semantics=sem, vmem_limit_bytes=VMEM_LIMIT)


def _ffn_kernel(*refs, final):
    if final:
        h_ref, g_ref, wg_ref, wu_ref, wd_ref, fg_ref, o_ref = refs
    else:
        h_ref, g_ref, wg_ref, wu_ref, wd_ref, o_ref = refs
    h = h_ref[...]
    xn = _rms(h, g_ref[...]).astype(BF16)
    acc = jnp.zeros_like(h)
    for c in range(0, D_FF, FF_CHUNK):
        gate = _dot(xn, wg_ref[:, c:c + FF_CHUNK])
        up = _dot(xn, wu_ref[:, c:c + FF_CHUNK])
        act = (jax.nn.silu(gate) * up).astype(BF16)
        acc = acc + _dot(act, wd_ref[c:c + FF_CHUNK, :])
    out = h + 0.5 * acc
    if final:
        out = _rms(out, fg_ref[...])
    o_ref[...] = out


def _ffn(h, gain, wg, wu, wd, final_gain=None):
    rows = h.shape[0]
    tm = min(TOKEN_TILE, rows)
    final = final_gain is not None
    row_spec = pl.BlockSpec((tm, D_MODEL), lambda i: (i, 0))
    in_specs = [row_spec, _const_spec((1, D_MODEL)), _const_spec((D_MODEL, D_FF)),
                _const_spec((D_MODEL, D_FF)), _const_spec((D_FF, D_MODEL))]
    args = [h, gain, wg, wu, wd]
    if final:
        in_specs.append(_const_spec((1, D_MODEL)))
        args.append(final_gain)
    return pl.pallas_call(
        functools.partial(_ffn_kernel, final=final),
        grid=(rows // tm,),
        in_specs=in_specs,
        out_specs=row_spec,
        out_shape=jax.ShapeDtypeStruct((rows, D_MODEL), F32),
        compiler_params=_params(("arbitrary",)),
        name="ffn_final" if final else "ffn",
    )(*args)


def _merge_kernel(h_ref, yabd_ref, yc_ref, g_ref, wgate_ref, wba_ref, wbb_ref, wbc_ref, wbd_ref,
                  wout_ref, o_ref):
    h = h_ref[...]
    xn = _rms(h, g_ref[...]).astype(BF16)
    ys = (yabd_ref[:, 0:BRANCH_W], yabd_ref[:, BRANCH_W:2 * BRANCH_W], yc_ref[...],
          yabd_ref[:, 2 * BRANCH_W:3 * BRANCH_W])
    wbs = (wba_ref, wbb_ref, wbc_ref, wbd_ref)
    merged = jnp.zeros_like(h)
    for i in range(4):
        logits = _dot(xn, wgate_ref[:, i * D_MODEL:(i + 1) * D_MODEL])
        merged = merged + jax.nn.sigmoid(logits) * _dot(ys[i], wbs[i][...])
    o_ref[...] = h + _dot(merged.astype(BF16), wout_ref[...])


def _merge(h, yabd, yc, gain, wgate, wba, wbb, wbc, wbd, wout):
    rows = h.shape[0]
    tm = min(TOKEN_TILE, rows)
    yc_w = MLA_HEADS * HEAD_PAD
    row_spec = pl.BlockSpec((tm, D_MODEL), lambda i: (i, 0))
    return pl.pallas_call(
        _merge_kernel,
        grid=(rows // tm,),
        in_specs=[row_spec,
                  pl.BlockSpec((tm, 3 * BRANCH_W), lambda i: (i, 0)),
                  pl.BlockSpec((tm, yc_w), lambda i: (i, 0)),
                  _const_spec((1, D_MODEL)),
                  _const_spec((D_MODEL, 4 * D_MODEL)),
                  _const_spec((BRANCH_W, D_MODEL)), _const_spec((BRANCH_W, D_MODEL)),
                  _const_spec((yc_w, D_MODEL)), _const_spec((BRANCH_W, D_MODEL)),
                  _const_spec((D_MODEL, D_MODEL))],
        out_specs=row_spec,
        out_shape=jax.ShapeDtypeStruct((rows, D_MODEL), F32),
        compiler_params=_params(("arbitrary",)),
        name="merge",
    )(h, yabd, yc, gain, wgate, wba, wbb, wbc, wbd, wout)


def _layer_norm_silu(x, g, b):
    mu = jnp.mean(x, axis=-1, keepdims=True)
    xc = x - mu
    var = jnp.mean(xc * xc, axis=-1, keepdims=True)
    return jax.nn.silu(xc * lax.rsqrt(var + LN_EPS) * g + b)


def _rope_heads(qa, qb, ta, tb):
    return jnp.concatenate(
        [qa[:, hd * HEAD_PAD:(hd + 1) * HEAD_PAD] * ta + qb[:, hd * HEAD_PAD:(hd + 1) * HEAD_PAD] * tb
         for hd in range(MLA_HEADS)], axis=1)


def _mix_in_prompt_kernel(h_ref, g_ref, w1_ref, scw_ref, gmn_ref, gmw_ref, gmb_ref, qn_ref, kvn_ref,
                          wqa_ref, wqb_ref, wkv_ref, cfw_ref, cfb_ref, lng_ref, lnb_ref,
                          tqa_ref, tqb_ref, tka_ref, tkb_ref,
                          q_ref, k_ref, v_ref, yabd_ref, lat_ref, kr_ref, sta_ref, std_ref,
                          bufa, bufd, *, tm):
    @pl.when(pl.program_id(1) == 0)
    def _():
        bufa[0:HALO_A, :] = jnp.zeros((HALO_A, BRANCH_W), F32)
        bufd[0:HALO_D, :] = jnp.zeros((HALO_D, BRANCH_W), F32)

    xn = _rms(h_ref[...], g_ref[...]).astype(BF16)

    za = _dot(xn, w1_ref[:, C_AH:C_BU])
    ga = za[:, BRANCH_W:2 * BRANCH_W] * za[:, 0:BRANCH_W]
    bufa[HALO_A:HALO_A + tm, :] = ga
    conv_a = (scw_ref[0:1, :] * bufa[HALO_A - 2:HALO_A - 2 + tm, :]
              + scw_ref[1:2, :] * bufa[HALO_A - 1:HALO_A - 1 + tm, :]
              + scw_ref[2:3, :] * ga)
    yabd_ref[:, 0:BRANCH_W] = (za[:, 2 * BRANCH_W:3 * BRANCH_W] * conv_a).astype(BF16)
    sta_ref[0] = bufa[HALO_A + tm - (SC_WIDTH - 1):HALO_A + tm, :]
    bufa[0:HALO_A, :] = bufa[tm:tm + HALO_A, :]

    zb = _dot(xn, w1_ref[:, C_BU:C_CQ])
    u = jax.nn.gelu(zb[:, 0:BRANCH_W])
    v = _rms(jax.nn.gelu(zb[:, BRANCH_W:2 * BRANCH_W]), gmn_ref[...]).astype(BF16)
    tril = (lax.broadcasted_iota(jnp.int32, (GM_CHUNK, GM_CHUNK), 0)
            >= lax.broadcasted_iota(jnp.int32, (GM_CHUNK, GM_CHUNK), 1))
    wms = [jnp.where(tril, gmw_ref[g], 0.0).astype(BF16) for g in range(GM_GROUPS)]
    group = lax.broadcasted_iota(jnp.int32, (GM_CHUNK, BRANCH_W), 1) // GM_GROUP_W
    for c in range(tm // GM_CHUNK):
        rows = slice(c * GM_CHUNK, (c + 1) * GM_CHUNK)
        vc = v[rows, :]
        mix = gmb_ref[...]
        for g in range(GM_GROUPS):
            mix = mix + _dot(wms[g], jnp.where(group == g, vc, jnp.zeros_like(vc)))
        yabd_ref[rows, BRANCH_W:2 * BRANCH_W] = (u[rows, :] * mix).astype(BF16)

    zc = _dot(xn, w1_ref[:, C_CQ:C_DV])
    cq = _rms(zc, qn_ref[...]).astype(BF16)
    q = _rope_heads(_dot(cq, wqa_ref[...]), _dot(cq, wqb_ref[...]), tqa_ref[...], tqb_ref[...])
    q_ref[...] = q.astype(BF16)
    zk = _dot(xn, w1_ref[:, C_KV:W1_COLS])
    lat = _rms(zk[:, 0:KV_RANK], kvn_ref[...])
    kr = zk[:, KV_RANK:KV_RANK + LANES] * tka_ref[...] + zk[:, KV_RANK + LANES:] * tkb_ref[...]
    lat_ref[...] = lat
    kr_ref[...] = kr[:, 0:QK_ROPE]
    one_lane = lax.broadcasted_iota(jnp.int32, kr.shape, 1) == QK_ROPE
    kv = _dot(jnp.concatenate([lat, jnp.where(one_lane, 1.0, kr)], axis=1).astype(BF16), wkv_ref[...])
    k_ref[...] = kv[:, 0:MLA_HEADS * HEAD_PAD].astype(BF16)
    v_ref[...] = kv[:, MLA_HEADS * HEAD_PAD:].astype(BF16)

    zd = _dot(xn, w1_ref[:, C_DV:C_KV])
    glu = zd[:, 0:BRANCH_W] * jax.nn.sigmoid(zd[:, BRANCH_W:2 * BRANCH_W])
    bufd[HALO_D:HALO_D + tm, :] = glu
    off = HALO_D - (CF_WIDTH - 1)
    conv_d = cfb_ref[...] + cfw_ref[0:1, :] * bufd[off:off + tm, :]
    for kk in range(1, CF_WIDTH):
        conv_d = conv_d + cfw_ref[kk:kk + 1, :] * bufd[off + kk:off + kk + tm, :]
    yabd_ref[:, 2 * BRANCH_W:3 * BRANCH_W] = _layer_norm_silu(conv_d, lng_ref[...], lnb_ref[...]).astype(BF16)
    std_ref[0] = bufd[HALO_D + tm - (CF_WIDTH - 1):HALO_D + tm, :]
    bufd[0:HALO_D, :] = bufd[tm:tm + HALO_D, :]


def _mix_in_prompt(h, n_seq, seq, lw, tabs):
    tm = min(TOKEN_TILE, seq)
    nt = seq // tm
    rows = n_seq * seq
    hp = MLA_HEADS * HEAD_PAD

    def rowspec(w):
        return pl.BlockSpec((tm, w), lambda b, j: (b * nt + j, 0))

    def tabspec():
        return pl.BlockSpec((tm, LANES), lambda b, j: (j, 0))

    in_specs = [rowspec(D_MODEL), _const_spec((1, D_MODEL)), _const_spec((D_MODEL, W1_COLS)),
                _const_spec((SC_WIDTH, BRANCH_W)), _const_spec((1, BRANCH_W)),
                _const_spec((GM_GROUPS, GM_CHUNK, GM_CHUNK)), _const_spec((GM_CHUNK, BRANCH_W)),
                _const_spec((1, Q_RANK)), _const_spec((1, KV_RANK)),
                _const_spec((Q_RANK, hp)), _const_spec((Q_RANK, hp)), _const_spec((KV_RANK + LANES, 2 * hp)),
                _const_spec((CF_WIDTH, BRANCH_W)), _const_spec((1, BRANCH_W)),
                _const_spec((1, BRANCH_W)), _const_spec((1, BRANCH_W)),
                tabspec(), tabspec(), tabspec(), tabspec()]
    out_specs = [rowspec(hp), rowspec(hp), rowspec(hp), rowspec(3 * BRANCH_W), rowspec(KV_RANK),
                 rowspec(QK_ROPE),
                 pl.BlockSpec((1, SC_WIDTH - 1, BRANCH_W), lambda b, j: (b, 0, 0)),
                 pl.BlockSpec((1, CF_WIDTH - 1, BRANCH_W), lambda b, j: (b, 0, 0))]
    out_shape = [jax.ShapeDtypeStruct((rows, hp), BF16), jax.ShapeDtypeStruct((rows, hp), BF16),
                 jax.ShapeDtypeStruct((rows, hp), BF16), jax.ShapeDtypeStruct((rows, 3 * BRANCH_W), BF16),
                 jax.ShapeDtypeStruct((rows, KV_RANK), F32), jax.ShapeDtypeStruct((rows, QK_ROPE), F32),
                 jax.ShapeDtypeStruct((n_seq, SC_WIDTH - 1, BRANCH_W), F32),
                 jax.ShapeDtypeStruct((n_seq, CF_WIDTH - 1, BRANCH_W), F32)]
    return pl.pallas_call(
        functools.partial(_mix_in_prompt_kernel, tm=tm),
        grid=(n_seq, nt),
        in_specs=in_specs,
        out_specs=out_specs,
        out_shape=out_shape,
        scratch_shapes=[pltpu.VMEM((HALO_A + tm, BRANCH_W), F32), pltpu.VMEM((HALO_D + tm, BRANCH_W), F32)],
        compiler_params=_params(("arbitrary", "arbitrary")),
        name="mix_in_prompt",
    )(h, lw["mix_gain"], lw["w1"], lw["sc_w"], lw["gm_gain"], lw["gm_w"], lw["gm_bias_full"],
      lw["q_gain"], lw["kv_gain"], lw["wqa"], lw["wqb"], lw["wkv"], lw["cf_w"], lw["cf_b"],
      lw["ln_g"], lw["ln_b"], tabs["tqa"], tabs["tqb"], tabs["tka"], tabs["tkb"])


def _mix_in_sample_kernel(h_ref, g_ref, w1_ref, scw_ref, gmn_ref, coef_ref, gmb_ref, qn_ref, kvn_ref,
                          wqa_ref, wqb_ref, wpa_ref, wpb_ref, wukt_ref, cfw_ref, cfb_ref, lng_ref, lnb_ref,
                          tqa_ref, tqb_ref, tka_ref, tkb_ref, sta_ref, std_ref,
                          qlat_ref, qpe_ref, yabd_ref, lat_ref, kr_ref, nsta_ref, glu_ref, v_ref,
                          *, ns, ts):
    def blk(x, t):
        return x[t * ns:(t + 1) * ns, :]

    xn = _rms(h_ref[...], g_ref[...]).astype(BF16)

    za = _dot(xn, w1_ref[:, C_AH:C_BU])
    ga = za[:, BRANCH_W:2 * BRANCH_W] * za[:, 0:BRANCH_W]
    full_a = [sta_ref[i] for i in range(SC_WIDTH - 1)] + [blk(ga, t) for t in range(ts)]
    for t in range(ts):
        conv = scw_ref[0:1, :] * full_a[t]
        for kk in range(1, SC_WIDTH):
            conv = conv + scw_ref[kk:kk + 1, :] * full_a[t + kk]
        yabd_ref[t * ns:(t + 1) * ns, 0:BRANCH_W] = (blk(za, t)[:, 2 * BRANCH_W:3 * BRANCH_W] * conv).astype(BF16)
    for i in range(SC_WIDTH - 1):
        nsta_ref[i] = full_a[ts + i]

    zb = _dot(xn, w1_ref[:, C_BU:C_CQ])
    u = jax.nn.gelu(zb[:, 0:BRANCH_W])
    v = _rms(jax.nn.gelu(zb[:, BRANCH_W:2 * BRANCH_W]), gmn_ref[...])
    v_ref[...] = v
    for i in range(ts):
        mix = gmb_ref[i:i + 1, :] + coef_ref[i * ts:i * ts + 1, :] * blk(v, 0)
        for j in range(1, i + 1):
            mix = mix + coef_ref[i * ts + j:i * ts + j + 1, :] * blk(v, j)
        yabd_ref[i * ns:(i + 1) * ns, BRANCH_W:2 * BRANCH_W] = (blk(u, i) * mix).astype(BF16)

    zc = _dot(xn, w1_ref[:, C_CQ:C_DV])
    cq = _rms(zc, qn_ref[...]).astype(BF16)
    q = _rope_heads(_dot(cq, wqa_ref[...]), _dot(cq, wqb_ref[...]), tqa_ref[...], tqb_ref[...]).astype(BF16)
    qlat_ref[...] = jnp.concatenate(
        [_dot(q[:, hd * HEAD_PAD:(hd + 1) * HEAD_PAD], wukt_ref[hd]) for hd in range(MLA_HEADS)],
        axis=1).astype(BF16)
    qpe = _rope_heads(_dot(cq, wpa_ref[...]), _dot(cq, wpb_ref[...]), tka_ref[...], tkb_ref[...])
    qpe_ref[...] = (qpe * ATTN_SCALE).astype(BF16)
    zk = _dot(xn, w1_ref[:, C_KV:W1_COLS])
    lat = _rms(zk[:, 0:KV_RANK], kvn_ref[...])
    kr = zk[:, KV_RANK:KV_RANK + LANES] * tka_ref[...] + zk[:, KV_RANK + LANES:] * tkb_ref[...]
    lat_ref[...] = lat
    kr_ref[...] = kr[:, 0:QK_ROPE]

    zd = _dot(xn, w1_ref[:, C_DV:C_KV])
    glu = zd[:, 0:BRANCH_W] * jax.nn.sigmoid(zd[:, BRANCH_W:2 * BRANCH_W])
    glu_ref[...] = glu
    full_d = [std_ref[i] for i in range(CF_WIDTH - 1)] + [blk(glu, t) for t in range(ts)]
    for t in range(ts):
        conv = cfb_ref[...] + cfw_ref[0:1, :] * full_d[t]
        for kk in range(1, CF_WIDTH):
            conv = conv + cfw_ref[kk:kk + 1, :] * full_d[t + kk]
        yabd_ref[t * ns:(t + 1) * ns, 2 * BRANCH_W:3 * BRANCH_W] = _layer_norm_silu(
            conv, lng_ref[...], lnb_ref[...]).astype(BF16)


def _mix_in_sample(h, ns, ts, lw, tabs, sta_t, std_t):
    rows = ns * ts
    hp = MLA_HEADS * HEAD_PAD
    args = (h, lw["mix_gain"], lw["w1"], lw["sc_w"], lw["gm_gain"], lw["gm_coef"], lw["gm_bias_rows"],
            lw["q_gain"], lw["kv_gain"], lw["wqa"], lw["wqb"], lw["wpa"], lw["wpb"], lw["wukt"],
            lw["cf_w"], lw["cf_b"], lw["ln_g"], lw["ln_b"],
            tabs["tqa"], tabs["tqb"], tabs["tka"], tabs["tkb"], sta_t, std_t)
    out_shape = [jax.ShapeDtypeStruct((rows, hp), BF16), jax.ShapeDtypeStruct((rows, hp), BF16),
                 jax.ShapeDtypeStruct((rows, 3 * BRANCH_W), BF16),
                 jax.ShapeDtypeStruct((rows, KV_RANK), F32), jax.ShapeDtypeStruct((rows, QK_ROPE), F32),
                 jax.ShapeDtypeStruct((SC_WIDTH - 1, ns, BRANCH_W), F32),
                 jax.ShapeDtypeStruct((rows, BRANCH_W), F32), jax.ShapeDtypeStruct((rows, BRANCH_W), F32)]

    def full(a):
        nd = a.ndim
        return pl.BlockSpec(a.shape, lambda i: (0,) * nd)

    return pl.pallas_call(
        functools.partial(_mix_in_sample_kernel, ns=ns, ts=ts),
        grid=(1,),
        in_specs=[full(a) for a in args],
        out_specs=[full(s) for s in out_shape],
        out_shape=out_shape,
        compiler_params=_params(("arbitrary",)),
        name="mix_in_sample",
    )(*args)


def _prompt_attn_kernel(qi_ref, ki_ref, q_ref, k_ref, v_ref, o_ref, m_s, acc_s, *, tile):
    step = pl.program_id(1)
    qi = qi_ref[step]
    ki = ki_ref[step]

    @pl.when(ki == 0)
    def _():
        m_s[...] = jnp.full(m_s.shape, -jnp.inf, F32)
        acc_s[...] = jnp.zeros(acc_s.shape, F32)

    def update(masked):
        if masked:
            keep = (lax.broadcasted_iota(jnp.int32, (tile, tile), 1)
                    <= lax.broadcasted_iota(jnp.int32, (tile, tile), 0))
        for hd in range(MLA_HEADS):
            cols = slice(hd * HEAD_PAD, (hd + 1) * HEAD_PAD)
            s = _dot_nt(q_ref[:, cols], k_ref[:, cols])
            if masked:
                s = jnp.where(keep, s, -jnp.inf)
            m_prev = m_s[hd]
            m_new = jnp.maximum(m_prev, jnp.max(s, axis=1, keepdims=True))
            alpha = jnp.exp2(m_prev - m_new)
            p = jnp.exp2(s - m_new)
            acc_s[hd] = alpha * acc_s[hd] + _dot(p.astype(BF16), v_ref[:, cols])
            m_s[hd] = m_new

    @pl.when(ki < qi)
    def _():
        update(False)

    @pl.when(ki == qi)
    def _():
        update(True)
        for hd in range(MLA_HEADS):
            acc = acc_s[hd]
            o_ref[:, hd * HEAD_PAD:(hd + 1) * HEAD_PAD] = (acc / acc[:, V_HEAD:V_HEAD + 1]).astype(BF16)


def _prompt_attn(q, k, v, n_seq, seq):
    tile = min(ATTN_TILE, seq)
    nt = seq // tile
    hp = MLA_HEADS * HEAD_PAD
    pairs = [(a, b) for a in range(nt) for b in range(a + 1)]
    qi_tab = jnp.asarray([p[0] for p in pairs], jnp.int32)
    ki_tab = jnp.asarray([p[1] for p in pairs], jnp.int32)
    q_spec = pl.BlockSpec((tile, hp), lambda b, s, qi, ki: (b * nt + qi[s], 0))
    k_spec = pl.BlockSpec((tile, hp), lambda b, s, qi, ki: (b * nt + ki[s], 0))
    return pl.pallas_call(
        functools.partial(_prompt_attn_kernel, tile=tile),
        grid_spec=pltpu.PrefetchScalarGridSpec(
            num_scalar_prefetch=2,
            grid=(n_seq, len(pairs)),
            in_specs=[q_spec, k_spec, k_spec],
            out_specs=q_spec,
            scratch_shapes=[pltpu.VMEM((MLA_HEADS, tile, 1), F32),
                            pltpu.VMEM((MLA_HEADS, tile, HEAD_PAD), F32)]),
        out_shape=jax.ShapeDtypeStruct((n_seq * seq, hp), BF16),
        compiler_params=_params(("arbitrary", "arbitrary")),
        name="prompt_attn",
    )(qi_tab, ki_tab, q, k, v)


def _decode_attn_kernel(pt_ref, qlat_ref, qpe_ref, latn_ref, krtn_ref, wuv_ref, *rest, pages, ts):
    lat_refs = rest[:pages]
    krt_refs = rest[pages:2 * pages]
    o_ref, lat_s, krt_s = rest[2 * pages:]
    rows = ts * MLA_HEADS
    past = pages * PAGE_SIZE

    for i in range(pages):
        lat_s[i * PAGE_SIZE:(i + 1) * PAGE_SIZE, :] = lat_refs[i][...].astype(BF16)
        krt_s[:, i * PAGE_SIZE:(i + 1) * PAGE_SIZE] = krt_refs[i][...].astype(BF16)
    n_new = latn_ref.shape[1]
    latn = jnp.concatenate([latn_ref[0], jnp.zeros((PAGE_SIZE - n_new, KV_RANK), F32)], axis=0)
    lat_s[past:past + PAGE_SIZE, :] = latn.astype(BF16)
    krt_s[:, past:past + PAGE_SIZE] = krtn_ref[0].astype(BF16)

    s = _dot_nt(qlat_ref[0], lat_s[...]) + _dot(qpe_ref[0][:, 0:QK_ROPE], krt_s[...])
    t_of_row = lax.broadcasted_iota(jnp.int32, s.shape, 0) // MLA_HEADS
    col = lax.broadcasted_iota(jnp.int32, s.shape, 1)
    s = jnp.where(col <= past + t_of_row, s, -jnp.inf)
    p = jnp.exp(s - jnp.max(s, axis=1, keepdims=True))
    o_lat = (_dot(p.astype(BF16), lat_s[...]) / jnp.sum(p, axis=1, keepdims=True)).astype(BF16)
    head_of_row = lax.broadcasted_iota(jnp.int32, (rows, HEAD_PAD), 0) % MLA_HEADS
    out = jnp.zeros((rows, HEAD_PAD), F32)
    for hd in range(MLA_HEADS):
        out = jnp.where(head_of_row == hd, _dot(o_lat, wuv_ref[hd]), out)
    o_ref[0] = out.astype(BF16)


def _decode_attn(layer, page_table, qlat, qpe, latn, krtn, wuv, cache_lat, cache_krt, ts):
    ns, pages = page_table.shape
    rows = ts * MLA_HEADS
    n_new = latn.shape[1]
    keys = (pages + 1) * PAGE_SIZE

    def seq_spec(r, w):
        return pl.BlockSpec((1, r, w), lambda n, pt: (n, 0, 0))

    def page_spec(r, w, i):
        return pl.BlockSpec((None, None, r, w), lambda n, pt: (layer, pt[n, i], 0, 0))

    in_specs = ([seq_spec(rows, KV_RANK), seq_spec(rows, LANES), seq_spec(n_new, KV_RANK),
                 seq_spec(QK_ROPE, PAGE_SIZE),
                 pl.BlockSpec((MLA_HEADS, KV_RANK, HEAD_PAD), lambda n, pt: (0, 0, 0))]
                + [page_spec(PAGE_SIZE, KV_RANK, i) for i in range(pages)]
                + [page_spec(QK_ROPE, PAGE_SIZE, i) for i in range(pages)])
    return pl.pallas_call(
        functools.partial(_decode_attn_kernel, pages=pages, ts=ts),
        grid_spec=pltpu.PrefetchScalarGridSpec(
            num_scalar_prefetch=1,
            grid=(ns,),
            in_specs=in_specs,
            out_specs=seq_spec(rows, HEAD_PAD),
            scratch_shapes=[pltpu.VMEM((keys, KV_RANK), BF16), pltpu.VMEM((QK_ROPE, keys), BF16)]),
        out_shape=jax.ShapeDtypeStruct((ns, rows, HEAD_PAD), BF16),
        compiler_params=_params(("arbitrary",)),
        name="decode_attn",
    )(page_table, qlat, qpe, latn, krtn, wuv, *([cache_lat] * pages), *([cache_krt] * pages))


def _pad_cols(x, width):
    return jnp.pad(x, ((0, 0), (0, width - x.shape[1])))


def _rot_cols(w):
    half = w.shape[-1] // 2
    return jnp.concatenate([-w[..., half:], w[..., :half]], axis=-1)


def _per_head(blocks):
    return jnp.concatenate([_pad_cols(b, HEAD_PAD) for b in blocks], axis=1)


def _pack_layer(l, p):
    w_in = p["w_in"][l]
    pts = [0]
    for s in (BRANCH_W,) * 5 + (Q_RANK, KV_RANK, QK_ROPE, BRANCH_W, BRANCH_W):
        pts.append(pts[-1] + s)
    a_h, a_c, a_b, b_u, b_v, c_q, c_kv, c_kr, d_v, d_g = (w_in[:, pts[i]:pts[i + 1]] for i in range(10))
    w1 = jnp.concatenate([a_h, a_c, a_b, b_u, b_v, c_q, d_v, d_g, c_kv,
                          _pad_cols(c_kr, LANES), _pad_cols(_rot_cols(c_kr), LANES)], axis=1).astype(BF16)
    w_uq = p["mla_w_uq"][l]
    w_uk = p["mla_w_uk"][l]
    w_uv = p["mla_w_uv"][l]
    zeros_nope = jnp.zeros((Q_RANK, QK_NOPE), F32)
    wqa = _per_head([w_uq[:, h, :] for h in range(MLA_HEADS)])
    wqb = _per_head([jnp.concatenate([zeros_nope, _rot_cols(w_uq[:, h, QK_NOPE:])], axis=1)
                     for h in range(MLA_HEADS)])
    wpa = _per_head([w_uq[:, h, QK_NOPE:] for h in range(MLA_HEADS)])
    wpb = _per_head([_rot_cols(w_uq[:, h, QK_NOPE:]) for h in range(MLA_HEADS)])
    wk = _per_head([w_uk[:, h, :] for h in range(MLA_HEADS)])
    wv = _per_head([w_uv[:, h, :] for h in range(MLA_HEADS)])
    place = jnp.zeros((LANES, HEAD_PAD), F32).at[jnp.arange(QK_ROPE), QK_NOPE + jnp.arange(QK_ROPE)].set(1.0)
    ones_col = jnp.zeros((LANES, HEAD_PAD), F32).at[QK_ROPE, V_HEAD].set(1.0)
    wkv = jnp.concatenate([jnp.concatenate([wk, wv], axis=1),
                           jnp.concatenate([jnp.tile(place, (1, MLA_HEADS)),
                                            jnp.tile(ones_col, (1, MLA_HEADS))], axis=1)], axis=0)
    wukt = jnp.stack([jnp.pad(w_uk[:, h, :].T, ((0, HEAD_PAD - QK_NOPE), (0, 0))) for h in range(MLA_HEADS)])
    wuv = jnp.stack([_pad_cols(w_uv[:, h, :], HEAD_PAD) for h in range(MLA_HEADS)])
    w_branch = p["w_branch"][l]
    wbc = jnp.concatenate([jnp.pad(w_branch[2, h * V_HEAD:(h + 1) * V_HEAD, :], ((0, HEAD_PAD - V_HEAD), (0, 0)))
                           for h in range(MLA_HEADS)], axis=0)
    gm_w = p["gm_spatial_w"][l]
    gm_b = p["gm_spatial_b"][l]
    return dict(
        f1=(p["ffn1_norm"][l][None, :], p["ffn1_w_gate"][l].astype(BF16), p["ffn1_w_up"][l].astype(BF16),
            p["ffn1_w_down"][l].astype(BF16)),
        f2=(p["ffn2_norm"][l][None, :], p["ffn2_w_gate"][l].astype(BF16), p["ffn2_w_up"][l].astype(BF16),
            p["ffn2_w_down"][l].astype(BF16)),
        mix_gain=p["mix_norm"][l][None, :], w1=w1, wgate=w_in[:, pts[10]:].astype(BF16),
        sc_w=p["sc_conv_w"][l], gm_gain=p["gm_norm"][l][None, :], gm_w=gm_w,
        gm_bias_full=jnp.repeat(gm_b.T, GM_GROUP_W, axis=1),
        q_gain=p["mla_q_norm"][l][None, :], kv_gain=p["mla_kv_norm"][l][None, :],
        wqa=wqa.astype(BF16), wqb=wqb.astype(BF16), wpa=wpa.astype(BF16), wpb=wpb.astype(BF16),
        wkv=wkv.astype(BF16), wukt=wukt.astype(BF16), wuv=wuv.astype(BF16),
        cf_w=p["cf_dw_w"][l], cf_b=p["cf_dw_b"][l][None, :], ln_g=p["cf_ln_g"][l][None, :],
        ln_b=p["cf_ln_b"][l][None, :],
        wba=w_branch[0].astype(BF16), wbb=w_branch[1].astype(BF16), wbc=wbc.astype(BF16),
        wbd=w_branch[3].astype(BF16), wout=p["w_out"][l].astype(BF16))


def _sample_gm(lw, ts):
    gm_w = lw["gm_w"][:, :ts, :ts]
    coef = jnp.repeat(gm_w.transpose(1, 2, 0).reshape(ts * ts, GM_GROUPS), GM_GROUP_W, axis=1)
    bias = lw["gm_bias_full"][:ts]
    return coef, bias


def _rope_tables(pos, q_scale):
    half = QK_ROPE // 2
    inv = ROPE_THETA ** (-jnp.arange(half, dtype=F32) / half)
    ang = pos.astype(F32)[:, None] * inv[None, :]
    cos2 = jnp.tile(jnp.cos(ang), (1, 2))
    sin2 = jnp.tile(jnp.sin(ang), (1, 2))
    n = pos.shape[0]
    ones = jnp.ones((n, QK_NOPE), F32)
    zeros = jnp.zeros((n, QK_NOPE), F32)
    return dict(tqa=_pad_cols(jnp.concatenate([ones, cos2], axis=1) * q_scale, LANES),
                tqb=_pad_cols(jnp.concatenate([zeros, sin2], axis=1) * q_scale, LANES),
                tka=_pad_cols(cos2, LANES), tkb=_pad_cols(sin2, LANES))


def kernel(x_prompt, x_sample, cache_mla_latent, cache_mla_krope, page_table, state_conv_a, state_conv_d,
           ffn1_norm, ffn1_w_gate, ffn1_w_up, ffn1_w_down, mix_norm, w_in, sc_conv_w, gm_norm,
           gm_spatial_w, gm_spatial_b, mla_q_norm, mla_kv_norm, mla_w_uq, mla_w_uk, mla_w_uv,
           cf_dw_w, cf_dw_b, cf_ln_g, cf_ln_b, w_branch, w_out, ffn2_norm, ffn2_w_gate, ffn2_w_up,
           ffn2_w_down, final_norm):
    p = dict(ffn1_norm=ffn1_norm, ffn1_w_gate=ffn1_w_gate, ffn1_w_up=ffn1_w_up, ffn1_w_down=ffn1_w_down,
             mix_norm=mix_norm, w_in=w_in, sc_conv_w=sc_conv_w, gm_norm=gm_norm, gm_spatial_w=gm_spatial_w,
             gm_spatial_b=gm_spatial_b, mla_q_norm=mla_q_norm, mla_kv_norm=mla_kv_norm, mla_w_uq=mla_w_uq,
             mla_w_uk=mla_w_uk, mla_w_uv=mla_w_uv, cf_dw_w=cf_dw_w, cf_dw_b=cf_dw_b, cf_ln_g=cf_ln_g,
             cf_ln_b=cf_ln_b, w_branch=w_branch, w_out=w_out, ffn2_norm=ffn2_norm, ffn2_w_gate=ffn2_w_gate,
             ffn2_w_up=ffn2_w_up, ffn2_w_down=ffn2_w_down)
    depth = w_in.shape[0]
    n_p, t_p, _ = x_prompt.shape
    n_s, t_s, _ = x_sample.shape
    past = page_table.shape[1] * PAGE_SIZE
    hp = MLA_HEADS * HEAD_PAD
    new_pad = -(-t_s // 8) * 8
    cache_krt = jnp.swapaxes(cache_mla_krope, 2, 3)

    tabs_p = _rope_tables(jnp.arange(t_p, dtype=jnp.int32), ATTN_SCALE * LOG2_E)
    tabs_s = {k: jnp.repeat(v, n_s, axis=0)
              for k, v in _rope_tables(past + jnp.arange(t_s, dtype=jnp.int32), ATTN_SCALE).items()}
    final_gain = final_norm[None, :]

    h_p = x_prompt.reshape(n_p * t_p, D_MODEL)
    h_s = x_sample.transpose(1, 0, 2).reshape(t_s * n_s, D_MODEL)

    def to_seq_major(x):
        return x.reshape(t_s, n_s, x.shape[-1]).transpose(1, 0, 2)

    outs = {k: [] for k in ("lat_p", "kr_p", "ca_p", "cd_p", "lat_s", "kr_s", "ca_s", "cd_s", "v_s")}
    for l in range(depth):
        lw = _pack_layer(l, p)
        lw["gm_coef"], lw["gm_bias_rows"] = _sample_gm(lw, t_s)
        last = l == depth - 1

        h_p = _ffn(h_p, *lw["f1"])
        q, k, v, yabd, lat, kr, sta, std = _mix_in_prompt(h_p, n_p, t_p, lw, tabs_p)
        yc = _prompt_attn(q, k, v, n_p, t_p)
        h_p = _merge(h_p, yabd, yc, lw["mix_gain"], lw["wgate"], lw["wba"], lw["wbb"], lw["wbc"], lw["wbd"],
                     lw["wout"])
        h_p = _ffn(h_p, *lw["f2"], final_gain=final_gain if last else None)
        outs["lat_p"].append(lat.reshape(n_p, t_p, KV_RANK))
        outs["kr_p"].append(kr.reshape(n_p, t_p, QK_ROPE))
        outs["ca_p"].append(sta)
        outs["cd_p"].append(std)

        h_s = _ffn(h_s, *lw["f1"])
        sta_t = state_conv_a[l].transpose(1, 0, 2)
        std_t = state_conv_d[l].transpose(1, 0, 2)
        qlat, qpe, yabd_s, lat_s, kr_s, nsta, glu, v_s = _mix_in_sample(h_s, n_s, t_s, lw, tabs_s, sta_t, std_t)

        def per_seq_rows(x):
            return x.reshape(t_s, n_s, MLA_HEADS, HEAD_PAD).transpose(1, 0, 2, 3).reshape(n_s, t_s * MLA_HEADS, HEAD_PAD)

        def new_rows(x):
            return jnp.pad(to_seq_major(x), ((0, 0), (0, new_pad - t_s), (0, 0)))

        krt_new = jnp.pad(to_seq_major(kr_s).transpose(0, 2, 1), ((0, 0), (0, 0), (0, PAGE_SIZE - t_s)))
        yc_s = _decode_attn(l, page_table, per_seq_rows(qlat), per_seq_rows(qpe), new_rows(lat_s), krt_new,
                            lw["wuv"], cache_mla_latent, cache_krt, t_s)
        yc_s = yc_s.reshape(n_s, t_s, MLA_HEADS, HEAD_PAD).transpose(1, 0, 2, 3).reshape(t_s * n_s, hp)
        h_s = _merge(h_s, yabd_s, yc_s, lw["mix_gain"], lw["wgate"], lw["wba"], lw["wbb"], lw["wbc"], lw["wbd"],
                     lw["wout"])
        h_s = _ffn(h_s, *lw["f2"], final_gain=final_gain if last else None)
        outs["lat_s"].append(to_seq_major(lat_s))
        outs["kr_s"].append(to_seq_major(kr_s))
        outs["ca_s"].append(nsta.transpose(1, 0, 2))
        outs["cd_s"].append(jnp.concatenate([state_conv_d[l][:, t_s:, :], to_seq_major(glu)], axis=1))
        outs["v_s"].append(to_seq_major(v_s))

    y_prompt = h_p.reshape(n_p, t_p, D_MODEL)
    y_sample = to_seq_major(h_s)
    return (y_prompt, y_sample,
            jnp.stack(outs["lat_p"]), jnp.stack(outs["kr_p"]), jnp.stack(outs["ca_p"]), jnp.stack(outs["cd_p"]),
            jnp.stack(outs["lat_s"]), jnp.stack(outs["kr_s"]), jnp.stack(outs["ca_s"]), jnp.stack(outs["cd_s"]),
            jnp.stack(outs["v_s"]))
```
